```python
import math
import jax, jax.numpy as jnp
from jax import lax
import numpy as np

D_MODEL = 1024
BATCH = 32
SEQ = 2048
DEPTH = 4
DEC_BATCH = 16
DEC_SEQ = 32
PAST_LEN = 1024

CHUNK = 64
N_EVEN = (DEPTH + 1) // 2
N_ODD = DEPTH // 2
CONV_W = 4
ALPHA = (2 * DEPTH) ** 0.25
BETA_INIT = (8 * DEPTH) ** -0.25
EPS = 1e-5
RG_W = D_MODEL
RG_BLOCKS = 8
RG_BW = RG_W // RG_BLOCKS
RG_C = 8.0
GD_H = 8
GD_DK = D_MODEL // GD_H
GD_DV = D_MODEL // GD_H
GD_QK = GD_H * GD_DK
GD_V = GD_H * GD_DV
GD_CONV = 2 * GD_QK + GD_V
ML_H = 4
ML_DK = D_MODEL // (2 * ML_H)
ML_DV = D_MODEL // ML_H
ML_QK = ML_H * ML_DK
ML_V = ML_H * ML_DV
HG_H = 8
HG_DK = D_MODEL // HG_H
HG_DV = D_MODEL // HG_H
HG_QK = HG_H * HG_DK
HG_V = HG_H * HG_DV
HG_BLOCK = 16
EVEN_SIZES = (RG_W, RG_W, GD_QK, GD_QK, GD_V, GD_H, GD_H, GD_V)
ODD_SIZES = (ML_QK, ML_QK, ML_V, ML_H, ML_H, ML_V, ML_V, HG_QK, HG_QK, HG_V, HG_V)
EVEN_IN = sum(EVEN_SIZES)
ODD_IN = sum(ODD_SIZES)
EVEN_OUT = RG_W + GD_V
ODD_OUT = ML_V + HG_V

kernel_name = "hybrid_rglru_gdn_mlstm_hgrn2_stream_step"


def _split(z, sizes):
    idx = np.cumsum(sizes)[:-1].tolist()
    return jnp.split(z, idx, axis=-1)


def _block(T, max_blk):
    b = min(max_blk, T)
    while T % b:
        b -= 1
    return b


def _to_blocks(z, C):
    B, T, H = z.shape[:3]
    z = z.reshape((B, T // C, C, H) + z.shape[3:])
    return jnp.moveaxis(z, (1, 3), (0, 2))


def _from_blocks(z):
    N, B, H, C = z.shape[:4]
    z = jnp.moveaxis(z, (0, 2), (1, 3))
    return z.reshape((B, N * C, H) + z.shape[4:])


def _layer_norm(x, g, b):
    xf = x.astype(jnp.float32)
    mu = jnp.mean(xf, axis=-1, keepdims=True)
    var = jnp.mean(jnp.square(xf - mu), axis=-1, keepdims=True)
    return ((xf - mu) * lax.rsqrt(var + EPS) * g + b).astype(x.dtype)


def _head_rms_norm(o, g):
    return o * lax.rsqrt(jnp.mean(jnp.square(o), axis=-1, keepdims=True) + EPS) * g


def _head_layer_norm(o, g):
    mu = jnp.mean(o, axis=-1, keepdims=True)
    var = jnp.mean(jnp.square(o - mu), axis=-1, keepdims=True)
    return (o - mu) * lax.rsqrt(var + EPS) * g


def _l2norm(z):
    return z * lax.rsqrt(jnp.sum(jnp.square(z), axis=-1, keepdims=True) + 1e-6)


def _causal_conv(u, buf, w):
    T = u.shape[1]
    ext = jnp.concatenate([buf.astype(u.dtype), u], axis=1)
    y = ext[:, 0:T] * w[0]
    for j in range(1, CONV_W):
        y = y + ext[:, j:j + T] * w[j]
    return y, ext[:, T:]


def _rglru(xc, h0, w_a, b_a, w_i, b_i, lam):
    B, T, W = xc.shape
    xb = xc.reshape(B, T, RG_BLOCKS, RG_BW)
    r = jax.nn.sigmoid(jnp.einsum('btgi,gij->btgj', xb, w_a).reshape(B, T, W) + b_a)
    i = jax.nn.sigmoid(jnp.einsum('btgi,gij->btgj', xb, w_i).reshape(B, T, W) + b_i)
    log_a = -RG_C * r * jax.nn.softplus(-lam)
    a = jnp.exp(log_a)
    u = jnp.sqrt(-jnp.expm1(2.0 * log_a)) * (i * xc)

    def step(h, au):
        h = au[0] * h + au[1]
        return h, h

    hT, hs = lax.scan(step, h0, (jnp.swapaxes(a, 0, 1), jnp.swapaxes(u, 0, 1)))
    return jnp.swapaxes(hs, 0, 1), hT


def _gated_delta(q, k, v, g, beta, S0):
    T = q.shape[1]
    C = _block(T, CHUNK)
    tril = jnp.tril(jnp.ones((C, C), dtype=bool))
    strict = jnp.tril(jnp.ones((C, C), dtype=bool), -1)
    eye = jnp.eye(C, dtype=jnp.float32)

    def body(S, blk):
        qc, kc, vc, gc, bc = blk
        G = jnp.cumsum(gc, axis=-1)
        dec = jnp.exp(jnp.where(tril, G[..., :, None] - G[..., None, :], -jnp.inf))
        A = jnp.where(strict, bc[..., :, None] * jnp.einsum('bhik,bhjk->bhij', kc, kc) * dec, 0.0)
        rhs = jnp.concatenate([kc * (bc * jnp.exp(G))[..., None], vc * bc[..., None]], axis=-1)
        X = lax.linalg.triangular_solve(eye + A, rhs, left_side=True, lower=True, unit_diagonal=True)
        w, u = X[..., :GD_DK], X[..., GD_DK:]
        dlt = u - jnp.einsum('bhck,bhkv->bhcv', w, S)
        qk = jnp.einsum('bhik,bhjk->bhij', qc, kc) * dec
        o = (jnp.einsum('bhck,bhkv->bhcv', qc * jnp.exp(G)[..., None], S)
             + jnp.einsum('bhij,bhjv->bhiv', qk, dlt))
        gL = G[..., -1:]
        S = (jnp.exp(gL)[..., None] * S
             + jnp.einsum('bhck,bhcv->bhkv', kc * jnp.exp(gL - G)[..., None], dlt))
        return S, o

    blocks = tuple(_to_blocks(z, C) for z in (q, k, v, g, beta))
    ST, o = lax.scan(body, S0, blocks)
    return _from_blocks(o), ST


def _mlstm(q, k, v, ig, lf, C0, n0, m0):
    T = q.shape[1]
    C = _block(T, CHUNK)
    tril = jnp.tril(jnp.ones((C, C), dtype=bool))

    def body(carry, blk):
        Cs, ns, m = carry
        qc, kc, vc, ic, fc = blk
        b = jnp.cumsum(fc, axis=-1)
        Dm = jnp.where(tril, b[..., :, None] - b[..., None, :] + ic[..., None, :], -jnp.inf)
        inter = b + m[..., None]
        mt = jnp.maximum(inter, jnp.max(Dm, axis=-1))
        Wt = jnp.exp(Dm - mt[..., None])
        sc = jnp.exp(inter - mt)
        qk = jnp.einsum('bhik,bhjk->bhij', qc, kc) * Wt
        num = sc[..., None] * jnp.einsum('bhck,bhkv->bhcv', qc, Cs) + jnp.einsum('bhij,bhjv->bhiv', qk, vc)
        den = sc * jnp.einsum('bhck,bhk->bhc', qc, ns) + jnp.sum(qk, axis=-1)
        h = num / jnp.maximum(jnp.abs(den), jnp.exp(-mt))[..., None]
        mL = mt[..., -1]
        wk = jnp.exp(b[..., -1:] - b + ic - mL[..., None])
        sL = jnp.exp(inter[..., -1] - mL)
        Cs = sL[..., None, None] * Cs + jnp.einsum('bhck,bhcv->bhkv', kc * wk[..., None], vc)
        ns = sL[..., None] * ns + jnp.einsum('bhc,bhck->bhk', wk, kc)
        return (Cs, ns, mL), h

    blocks = tuple(_to_blocks(z, C) for z in (q, k, v, ig, lf))
    (CT, nT, mT), h = lax.scan(body, (C0, n0, m0), blocks)
    return _from_blocks(h), CT, nT, mT


def _hgrn2(q, k, v, g, S0):
    T = q.shape[1]
    C = _block(T, HG_BLOCK)
    tril = jnp.tril(jnp.ones((C, C), dtype=bool))[..., None]

    def body(S, blk):
        qc, kc, vc, gc = blk
        G = jnp.cumsum(gc, axis=2)
        dec = jnp.exp(jnp.where(tril, G[:, :, :, None, :] - G[:, :, None, :, :], -jnp.inf))
        att = jnp.einsum('bhijk,bhjk->bhij', qc[:, :, :, None, :] * dec, kc)
        o = (jnp.einsum('bhck,bhkv->bhcv', qc * jnp.exp(G), S)
             + jnp.einsum('bhij,bhjv->bhiv', att, vc))
        GL = G[:, :, -1:]
        S = (jnp.exp(GL[:, :, 0])[..., None] * S
             + jnp.einsum('bhck,bhcv->bhkv', kc * jnp.exp(GL - G), vc))
        return S, o

    blocks = tuple(_to_blocks(z, C) for z in (q, k, v, g))
    ST, o = lax.scan(body, S0, blocks)
    return _from_blocks(o), ST


def setup_inputs(seed: int = 0) -> dict:
    key = jax.random.key(seed)
    keys = jax.random.split(key, 40)
    counter = [0]
    f32 = jnp.float32

    def nxt():
        k = keys[counter[0]]
        counter[0] += 1
        return k

    def nrm(shape, scale):
        return jax.random.normal(nxt(), shape, f32) * scale

    x_prompt = nrm((BATCH, SEQ, D_MODEL), 1.0)
    x_sample = nrm((DEC_BATCH, DEC_SEQ, D_MODEL), 1.0)
    state_rglru_h = nrm((N_EVEN, DEC_BATCH, RG_W), 0.5)
    state_rglru_conv = nrm((N_EVEN, DEC_BATCH, CONV_W - 1, RG_W), 1.0)
    state_gdn_S = nrm((N_EVEN, DEC_BATCH, GD_H, GD_DK, GD_DV), GD_DK ** -0.5)
    state_gdn_conv = nrm((N_EVEN, DEC_BATCH, CONV_W - 1, GD_CONV), 1.0)
    state_mlstm_C = nrm((N_ODD, DEC_BATCH, ML_H, ML_DK, ML_DV), 0.1)
    state_mlstm_n = jnp.abs(nrm((N_ODD, DEC_BATCH, ML_H, ML_DK), 0.5))
    state_mlstm_m = nrm((N_ODD, DEC_BATCH, ML_H), 1.0)
    state_hgrn_S = nrm((N_ODD, DEC_BATCH, HG_H, HG_DK, HG_DV), 0.5)

    w_in_even = nrm((N_EVEN, D_MODEL, EVEN_IN), D_MODEL ** -0.5)
    rg_conv_w = nrm((N_EVEN, CONV_W, RG_W), CONV_W ** -0.5)
    rg_conv_b = nrm((N_EVEN, RG_W), 0.01)
    rg_w_a = nrm((N_EVEN, RG_BLOCKS, RG_BW, RG_BW), RG_BW ** -0.5)
    rg_b_a = nrm((N_EVEN, RG_W), 0.01)
    rg_w_i = nrm((N_EVEN, RG_BLOCKS, RG_BW, RG_BW), RG_BW ** -0.5)
    rg_b_i = nrm((N_EVEN, RG_W), 0.01)
    a_c = jax.random.uniform(nxt(), (N_EVEN, RG_W), f32, 0.9, 0.999)
    a_base = a_c ** (1.0 / RG_C)
    rg_lambda = jnp.log(a_base) - jnp.log1p(-a_base)
    gdn_conv_w = nrm((N_EVEN, CONV_W, GD_CONV), CONV_W ** -0.5)
    gdn_a_log = jnp.log(jax.random.uniform(nxt(), (N_EVEN, GD_H), f32, 1.0, 16.0))
    dt0 = jnp.exp(jax.random.uniform(nxt(), (N_EVEN, GD_H), f32, math.log(1e-3), math.log(1e-1)))
    gdn_dt_bias = dt0 + jnp.log(-jnp.expm1(-dt0))
    gdn_norm_g = 1.0 + nrm((N_EVEN, GD_DV), 0.01)
    w_out_even = nrm((N_EVEN, EVEN_OUT, D_MODEL), EVEN_OUT ** -0.5 * BETA_INIT)
    ln_even_g = 1.0 + nrm((N_EVEN, D_MODEL), 0.01)
    ln_even_b = nrm((N_EVEN, D_MODEL), 0.01)

    w_in_odd = nrm((N_ODD, D_MODEL, ODD_IN), D_MODEL ** -0.5)
    ml_b_i = nrm((N_ODD, ML_H), 0.1)
    ml_b_f = jnp.linspace(3.0, 6.0, ML_H, dtype=f32) + nrm((N_ODD, ML_H), 0.1)
    ml_norm_g = 1.0 + nrm((N_ODD, ML_DV), 0.01)
    hg_lb = nrm((N_ODD, HG_QK), 0.1)
    hg_norm_g = 1.0 + nrm((N_ODD, HG_DV), 0.01)
    w_out_odd = nrm((N_ODD, ODD_OUT, D_MODEL), ODD_OUT ** -0.5 * BETA_INIT)
    ln_odd_g = 1.0 + nrm((N_ODD, D_MODEL), 0.01)
    ln_odd_b = nrm((N_ODD, D_MODEL), 0.01)

    return dict(x_prompt=x_prompt, x_sample=x_sample,
                state_rglru_h=state_rglru_h, state_rglru_conv=state_rglru_conv,
                state_gdn_S=state_gdn_S, state_gdn_conv=state_gdn_conv,
                state_mlstm_C=state_mlstm_C, state_mlstm_n=state_mlstm_n, state_mlstm_m=state_mlstm_m,
                state_hgrn_S=state_hgrn_S,
                w_in_even=w_in_even, rg_conv_w=rg_conv_w, rg_conv_b=rg_conv_b,
                rg_w_a=rg_w_a, rg_b_a=rg_b_a, rg_w_i=rg_w_i, rg_b_i=rg_b_i, rg_lambda=rg_lambda,
                gdn_conv_w=gdn_conv_w, gdn_a_log=gdn_a_log, gdn_dt_bias=gdn_dt_bias, gdn_norm_g=gdn_norm_g,
                w_out_even=w_out_even, ln_even_g=ln_even_g, ln_even_b=ln_even_b,
                w_in_odd=w_in_odd, ml_b_i=ml_b_i, ml_b_f=ml_b_f, ml_norm_g=ml_norm_g,
                hg_lb=hg_lb, hg_norm_g=hg_norm_g, w_out_odd=w_out_odd, ln_odd_g=ln_odd_g, ln_odd_b=ln_odd_b)


def reference(x_prompt, x_sample, state_rglru_h, state_rglru_conv, state_gdn_S, state_gdn_conv,
              state_mlstm_C, state_mlstm_n, state_mlstm_m, state_hgrn_S,
              w_in_even, rg_conv_w, rg_conv_b, rg_w_a, rg_b_a, rg_w_i, rg_b_i, rg_lambda,
              gdn_conv_w, gdn_a_log, gdn_dt_bias, gdn_norm_g, w_out_even, ln_even_g, ln_even_b,
              w_in_odd, ml_b_i, ml_b_f, ml_norm_g, hg_lb, hg_norm_g, w_out_odd, ln_odd_g, ln_odd_b):
    f32 = jnp.float32
    lb_cum = jnp.cumsum(jax.nn.softmax(hg_lb.astype(f32), axis=0), axis=0)
    hg_lower = lb_cum - lb_cum[0]

    def even_layer(x, j, h0, cbuf_a, S0, cbuf_b):
        B, T, _ = x.shape
        dt = x.dtype
        z = jnp.einsum('btd,de->bte', x, w_in_even[j]).astype(f32)
        xa, ga, qb, kb, vb, bb, ab, gb = _split(z, EVEN_SIZES)
        xa, cbuf_a = _causal_conv(xa, cbuf_a, rg_conv_w[j])
        xa = xa + rg_conv_b[j]
        ya, hT = _rglru(xa, h0.astype(f32), rg_w_a[j].astype(f32), rg_b_a[j], rg_w_i[j].astype(f32),
                        rg_b_i[j], rg_lambda[j].astype(f32))
        ya = ya * jax.nn.silu(ga)
        qkv, cbuf_b = _causal_conv(jnp.concatenate([qb, kb, vb], axis=-1), cbuf_b, gdn_conv_w[j])
        qkv = jax.nn.silu(qkv)
        qb, kb, vb = _split(qkv, (GD_QK, GD_QK, GD_V))
        q = _l2norm(qb.reshape(B, T, GD_H, GD_DK)) * GD_DK ** -0.5
        k = _l2norm(kb.reshape(B, T, GD_H, GD_DK))
        v = vb.reshape(B, T, GD_H, GD_DV)
        beta = jax.nn.sigmoid(bb)
        g = -jnp.exp(gdn_a_log[j].astype(f32)) * jax.nn.softplus(ab + gdn_dt_bias[j])
        ob, ST = _gated_delta(q, k, v, g, beta, S0.astype(f32))
        yb = _head_rms_norm(ob, gdn_norm_g[j]).reshape(B, T, GD_V) * jax.nn.silu(gb)
        y = jnp.einsum('bte,ed->btd', jnp.concatenate([ya, yb], axis=-1).astype(dt), w_out_even[j])
        x = _layer_norm(ALPHA * x + y, ln_even_g[j], ln_even_b[j])
        return x, hT.astype(dt), cbuf_a.astype(dt), ST.astype(dt), cbuf_b.astype(dt)

    def odd_layer(x, j, C0, n0, m0, S0):
        B, T, _ = x.shape
        dt = x.dtype
        z = jnp.einsum('btd,de->bte', x, w_in_odd[j]).astype(f32)
        mq, mk, mv, mi, mf, mo, mg, hq, hf, hi, hgate = _split(z, ODD_SIZES)
        q = mq.reshape(B, T, ML_H, ML_DK) * ML_DK ** -0.5
        k = mk.reshape(B, T, ML_H, ML_DK)
        v = mv.reshape(B, T, ML_H, ML_DV)
        lf = jax.nn.log_sigmoid(mf + ml_b_f[j])
        hc, CT, nT, mT = _mlstm(q, k, v, mi + ml_b_i[j], lf, C0.astype(f32), n0.astype(f32), m0.astype(f32))
        yc = jax.nn.sigmoid(mo) * _head_layer_norm(hc, ml_norm_g[j]).reshape(B, T, ML_V) * jax.nn.silu(mg)
        lb = hg_lower[j]
        logf = jnp.logaddexp(jnp.log(lb), jnp.log1p(-lb) + jax.nn.log_sigmoid(hf))
        kf = -jnp.expm1(logf)
        qh = jax.nn.silu(hq) * HG_DK ** -0.5
        oh, ST = _hgrn2(qh.reshape(B, T, HG_H, HG_DK), kf.reshape(B, T, HG_H, HG_DK),
                        hi.reshape(B, T, HG_H, HG_DV), logf.reshape(B, T, HG_H, HG_DK), S0.astype(f32))
        yd = _head_rms_norm(oh, hg_norm_g[j]).reshape(B, T, HG_V) * jax.nn.silu(hgate)
        y = jnp.einsum('bte,ed->btd', jnp.concatenate([yc, yd], axis=-1).astype(dt), w_out_odd[j])
        x = _layer_norm(ALPHA * x + y, ln_odd_g[j], ln_odd_b[j])
        return x, CT.astype(dt), nT.astype(dt), mT.astype(dt), ST.astype(dt)

    def run(x, rg_h, rg_cv, gd_S, gd_cv, ml_C, ml_n, ml_m, hg_S):
        o_rg_h, o_rg_cv, o_gd_S, o_gd_cv = [], [], [], []
        o_ml_C, o_ml_n, o_ml_m, o_hg_S = [], [], [], []
        for l in range(DEPTH):
            j = l // 2
            if l % 2 == 0:
                x, h, cva, S, cvb = even_layer(x, j, rg_h[j], rg_cv[j], gd_S[j], gd_cv[j])
                o_rg_h.append(h); o_rg_cv.append(cva); o_gd_S.append(S); o_gd_cv.append(cvb)
            else:
                x, Cm, nm, mm, Sh = odd_layer(x, j, ml_C[j], ml_n[j], ml_m[j], hg_S[j])
                o_ml_C.append(Cm); o_ml_n.append(nm); o_ml_m.append(mm); o_hg_S.append(Sh)
        return (x, jnp.stack(o_rg_h), jnp.stack(o_rg_cv), jnp.stack(o_gd_S), jnp.stack(o_gd_cv),
                jnp.stack(o_ml_C), jnp.stack(o_ml_n), jnp.stack(o_ml_m), jnp.stack(o_hg_S))

    Bp = x_prompt.shape[0]
    dt = x_prompt.dtype
    (y_prompt, p_rglru_h, p_rglru_conv, p_gdn_S, p_gdn_conv,
     p_mlstm_C, p_mlstm_n, p_mlstm_m, p_hgrn_S) = run(
        x_prompt,
        jnp.zeros((N_EVEN, Bp, RG_W), dt),
        jnp.zeros((N_EVEN, Bp, CONV_W - 1, RG_W), dt),
        jnp.zeros((N_EVEN, Bp, GD_H, GD_DK, GD_DV), dt),
        jnp.zeros((N_EVEN, Bp, CONV_W - 1, GD_CONV), dt),
        jnp.zeros((N_ODD, Bp, ML_H, ML_DK, ML_DV), dt),
        jnp.zeros((N_ODD, Bp, ML_H, ML_DK), dt),
        jnp.zeros((N_ODD, Bp, ML_H), dt),
        jnp.zeros((N_ODD, Bp, HG_H, HG_DK, HG_DV), dt))
    (y_sample, s_rglru_h, s_rglru_conv, s_gdn_S, s_gdn_conv,
     s_mlstm_C, s_mlstm_n, s_mlstm_m, s_hgrn_S) = run(
        x_sample, state_rglru_h, state_rglru_conv, state_gdn_S, state_gdn_conv,
        state_mlstm_C, state_mlstm_n, state_mlstm_m, state_hgrn_S)
    return (y_prompt, y_sample,
            p_rglru_h, p_rglru_conv, p_gdn_S, p_gdn_conv, p_mlstm_C, p_mlstm_n, p_mlstm_m, p_hgrn_S,
            s_rglru_h, s_rglru_conv, s_gdn_S, s_gdn_conv, s_mlstm_C, s_mlstm_n, s_mlstm_m, s_hgrn_S)
```

```python
import functools
import math

import jax
import jax.numpy as jnp
from jax import lax
from jax.experimental import pallas as pl
from jax.experimental.pallas import tpu as pltpu

F32 = jnp.float32
BF16 = jnp.bfloat16

D_MODEL = 1024
DEPTH = 4
CONV_W = 4
ALPHA = (2 * DEPTH) ** 0.25
EPS = 1e-5
RG_W = D_MODEL
RG_BLOCKS = 8
RG_BW = RG_W // RG_BLOCKS
RG_C = 8.0
GD_H = 8
GD_DK = D_MODEL // GD_H
GD_DV = D_MODEL // GD_H
GD_CONV = 3 * D_MODEL
ML_H = 4
ML_DK = D_MODEL // (2 * ML_H)
ML_DV = D_MODEL // ML_H
ML_QK = ML_H * ML_DK
HG_H = 8
HG_DK = D_MODEL // HG_H
HG_DV = D_MODEL // HG_H
HG_BLOCK = 16
CHUNK = 64

LANES = 128
SUBLANES = 8
CONV_PAD = SUBLANES
TIME_BLOCK = 256
VMEM_LIMIT_BYTES = 56 * 1024 * 1024

E_XA, E_GB, E_GA, E_Q, E_K, E_V, E_SM = 0, 1024, 2048, 3072, 4096, 5120, 6144
EVEN_COLS = E_SM + LANES
O_MO, O_HGATE, O_MG, O_MQ, O_MK, O_MV, O_HQ, O_HF, O_HI, O_SM = (
    0, 1024, 2048, 3072, 3584, 4096, 5120, 6144, 7168, 8192)
ODD_COLS = O_SM + LANES


def _dot(a, b):
    return jnp.dot(a.astype(BF16), b.astype(BF16), preferred_element_type=F32)


def _dot_nt(a, b):
    return lax.dot_general(a.astype(BF16), b.astype(BF16), (((1,), (1,)), ((), ())),
                           preferred_element_type=F32)


def _dot_tn(a, b):
    return lax.dot_general(a.astype(BF16), b.astype(BF16), (((0,), (0,)), ((), ())),
                           preferred_element_type=F32)


def _split2(x):
    hi = x.astype(BF16)
    lo = (x - hi.astype(F32)).astype(BF16)
    return hi, lo


def _dot_x3(a, b):
    a_hi, a_lo = _split2(a)
    b_hi, b_lo = _split2(b)
    acc = jnp.dot(a_hi, b_hi, preferred_element_type=F32)
    acc += jnp.dot(a_hi, b_lo, preferred_element_type=F32)
    acc += jnp.dot(a_lo, b_hi, preferred_element_type=F32)
    return acc


def _rows_from_cols(sel, m):
    hi = m.astype(BF16)
    r1 = m - hi.astype(F32)
    mid = r1.astype(BF16)
    lo = (r1 - mid.astype(F32)).astype(BF16)
    dn = (((1,), (1,)), ((), ()))
    out = lax.dot_general(sel, hi, dn, preferred_element_type=F32)
    out += lax.dot_general(sel, mid, dn, preferred_element_type=F32)
    out += lax.dot_general(sel, lo, dn, preferred_element_type=F32)
    return out


def _selector(lane0):
    r = lax.broadcasted_iota(jnp.int32, (SUBLANES, LANES), 0)
    c = lax.broadcasted_iota(jnp.int32, (SUBLANES, LANES), 1)
    return (c == r + lane0).astype(BF16)


def _sigmoid(x):
    return jax.nn.sigmoid(x)


def _silu(x):
    return x * jax.nn.sigmoid(x)


def _softplus(x):
    return jnp.maximum(x, 0.0) + jnp.log1p(jnp.exp(-jnp.abs(x)))


def _log_sigmoid(x):
    return jnp.minimum(x, 0.0) - jnp.log1p(jnp.exp(-jnp.abs(x)))


def _chunk_cumsum(x, chunk):
    n = x.shape[0]
    pos = lax.broadcasted_iota(jnp.int32, (n, 1), 0) % chunk
    d = 1
    while d < chunk:
        x = x + jnp.where(pos >= d, pltpu.roll(x, d, 0), 0.0)
        d *= 2
    return x


def _causal_conv(ext_ref, u, w_ref, tb):
    ext_ref[CONV_PAD:CONV_PAD + tb, :] = u
    base = CONV_PAD - (CONV_W - 1)
    y = ext_ref[base:base + tb, :] * w_ref[0:1, :]
    for j in range(1, CONV_W):
        y = y + ext_ref[base + j:base + j + tb, :] * w_ref[j:j + 1, :]
    ext_ref[base:CONV_PAD, :] = ext_ref[base + tb:CONV_PAD + tb, :]
    return y


def _layer_norm_rows(x, g, b):
    mu = jnp.mean(x, axis=-1, keepdims=True)
    xc = x - mu
    var = jnp.mean(xc * xc, axis=-1, keepdims=True)
    return xc * lax.rsqrt(var + EPS) * g + b


def _tri_masks(c):
    r = lax.broadcasted_iota(jnp.int32, (c, c), 0)
    s = lax.broadcasted_iota(jnp.int32, (c, c), 1)
    return s <= r, s < r, (s == r).astype(F32)


def _unit_lower_inverse(a, eye):
    c = a.shape[0]
    p = eye - a
    q = _dot_x3(a, a)
    span = 2
    while span < c:
        p = p + _dot_x3(p, q)
        span *= 2
        if span < c:
            q = _dot_x3(q, q)
    return p


def _even_kernel(*refs, tb, chunk, has_state):
    (x_ref, win_ref, cwa_ref, cba_ref, wai_ref, bai_ref, lam_ref, cwb_ref, gdp_ref, gng_ref,
     wout_ref, lng_ref, lnb_ref) = refs[:13]
    pos = 13
    if has_state:
        h0_ref, cva0_ref, s0_ref, cvb0_ref = refs[pos:pos + 4]
        pos += 4
    xo_ref, ho_ref, cvao_ref, so_ref, cvbo_ref = refs[pos:pos + 5]
    pos += 5
    z_ref, exta_ref, extb_ref, h_ref, s_ref, g_ref, beta_ref = refs[pos:pos + 7]

    t = pl.program_id(1)
    nt = pl.num_programs(1)
    base = CONV_PAD - (CONV_W - 1)

    @pl.when(t == 0)
    def _init():
        exta_ref[0:CONV_PAD, :] = jnp.zeros((CONV_PAD, RG_W), F32)
        extb_ref[0:CONV_PAD, :] = jnp.zeros((CONV_PAD, GD_CONV), F32)
        if has_state:
            h_ref[...] = h0_ref[...]
            exta_ref[base:CONV_PAD, :] = cva0_ref[...]
            extb_ref[base:CONV_PAD, :] = cvb0_ref[...]
            s_ref[...] = s0_ref[...]
        else:
            h_ref[...] = jnp.zeros_like(h_ref)
            s_ref[...] = jnp.zeros_like(s_ref)

    x = x_ref[...]
    xb = x.astype(BF16)
    ncol = 512
    for n0 in range(0, EVEN_COLS, ncol):
        n1 = min(n0 + ncol, EVEN_COLS)
        z_ref[:, n0:n1] = jnp.dot(xb, win_ref[:, n0:n1], preferred_element_type=F32)

    xa = _causal_conv(exta_ref, z_ref[:, E_XA:E_XA + RG_W], cwa_ref, tb) + cba_ref[...]
    gates = []
    for g in range(RG_BLOCKS):
        gates.append(_dot(xa[:, g * RG_BW:(g + 1) * RG_BW], wai_ref[g]))
    pre_r = jnp.concatenate([gt[:, :RG_BW] for gt in gates], axis=-1) + bai_ref[0:1, :]
    pre_i = jnp.concatenate([gt[:, RG_BW:] for gt in gates], axis=-1) + bai_ref[1:2, :]
    r = _sigmoid(pre_r)
    i = _sigmoid(pre_i)
    log_a = (-RG_C) * r * _softplus(-lam_ref[...])
    a = jnp.exp(log_a)
    th = jnp.tanh(log_a)
    u = jnp.sqrt((-2.0) * th / (1.0 - th)) * (i * xa)
    row = lax.broadcasted_iota(jnp.int32, (tb, 1), 0)
    d = 1
    while d < tb:
        m = row >= d
        u = jnp.where(m, a * pltpu.roll(u, d, 0) + u, u)
        a = jnp.where(m, a * pltpu.roll(a, d, 0), a)
        d *= 2
    hs = u + a * h_ref[...]
    h_ref[...] = hs[tb - 1:tb, :]
    z_ref[:, E_XA:E_XA + RG_W] = hs * _silu(z_ref[:, E_GA:E_GA + RG_W])

    qkv = _silu(_causal_conv(extb_ref, z_ref[:, E_Q:E_Q + GD_CONV], cwb_ref, tb))
    z_ref[:, E_Q:E_Q + GD_CONV] = qkv
    sm = z_ref[:, E_SM:E_SM + LANES]
    beta_ref[...] = _sigmoid(sm)
    g_all = -jnp.exp(gdp_ref[0:1, :]) * _softplus(sm + gdp_ref[1:2, :])
    g_ref[...] = _chunk_cumsum(g_all, chunk)
    sel = _selector(GD_H)
    tril, strict, eye = _tri_masks(chunk)

    def chunk_body(c, carry):
        r0 = pl.multiple_of(c * chunk, chunk)
        rows = pl.ds(r0, chunk)
        gblk = g_ref[rows, :]
        grow = _rows_from_cols(sel, gblk)
        bblk = beta_ref[rows, :]
        for h in range(GD_H):
            lq = slice(E_Q + h * GD_DK, E_Q + (h + 1) * GD_DK)
            lk = slice(E_K + h * GD_DK, E_K + (h + 1) * GD_DK)
            lv = slice(E_V + h * GD_DV, E_V + (h + 1) * GD_DV)
            qh = z_ref[rows, lq]
            kh = z_ref[rows, lk]
            vh = z_ref[rows, lv]
            qh = qh * lax.rsqrt(jnp.sum(qh * qh, axis=-1, keepdims=True) + 1e-6) * (GD_DK ** -0.5)
            kh = kh * lax.rsqrt(jnp.sum(kh * kh, axis=-1, keepdims=True) + 1e-6)
            gc = gblk[:, GD_H + h:GD_H + h + 1]
            gr = grow[h:h + 1, :]
            bc = bblk[:, h:h + 1]
            dec = jnp.exp(jnp.where(tril, gc - gr, -jnp.inf))
            kb = kh.astype(BF16)
            amat = jnp.where(strict, bc * _dot_nt(kb, kb) * dec, 0.0)
            eg = jnp.exp(gc)
            rhs = jnp.concatenate([kh * (bc * eg), vh * bc], axis=-1)
            tinv = _unit_lower_inverse(amat, eye)
            xs = _dot(tinv, rhs)
            w = xs[:, :GD_DK]
            uu = xs[:, GD_DK:]
            s_old = s_ref[h]
            sb = s_old.astype(BF16)
            dlt = uu - _dot(w, sb)
            qk = _dot_nt(qh, kb) * dec
            o = _dot(qh * eg, sb) + _dot(qk, dlt)
            gl = gc[chunk - 1:chunk, :]
            s_ref[h] = jnp.exp(gl) * s_old + _dot_tn(kh * jnp.exp(gl - gc), dlt)
            on = o * lax.rsqrt(jnp.mean(o * o, axis=-1, keepdims=True) + EPS) * gng_ref[...]
            lg = slice(E_GB + h * GD_DV, E_GB + (h + 1) * GD_DV)
            z_ref[rows, lg] = on * _silu(z_ref[rows, lg])
        return carry

    lax.fori_loop(0, tb // chunk, chunk_body, 0)

    y = jnp.dot(z_ref[:, 0:2 * D_MODEL].astype(BF16), wout_ref[...], preferred_element_type=F32)
    xo_ref[...] = _layer_norm_rows(ALPHA * x + y, lng_ref[...], lnb_ref[...])

    @pl.when(t == nt - 1)
    def _fin():
        ho_ref[...] = h_ref[...]
        cvao_ref[...] = exta_ref[base:CONV_PAD, :]
        cvbo_ref[...] = extb_ref[base:CONV_PAD, :]
        so_ref[...] = s_ref[...]


def _odd_kernel(*refs, tb, chunk, layer, has_state):
    (x_ref, win_ref, mlp_ref, mng_ref, hlb_ref, hng_ref, wout_ref, lng_ref, lnb_ref) = refs[:9]
    pos = 9
    if has_state:
        c0_ref, n0_ref, m0_ref, hs0_ref = refs[pos:pos + 4]
        pos += 4
    xo_ref, co_ref, no_ref, mo_ref, hso_ref = refs[pos:pos + 5]
    pos += 5
    z_ref, c_ref, n_ref, m_ref, st_ref, gi_ref, bb_ref = refs[pos:pos + 7]

    t = pl.program_id(1)
    nt = pl.num_programs(1)

    @pl.when(t == 0)
    def _init():
        if has_state:
            c_ref[...] = c0_ref[...]
            n_ref[...] = n0_ref[...]
            m_ref[...] = m0_ref[...]
            for h in range(HG_H):
                st_ref[h] = hs0_ref[h].T
        else:
            c_ref[...] = jnp.zeros_like(c_ref)
            n_ref[...] = jnp.zeros_like(n_ref)
            m_ref[...] = jnp.zeros_like(m_ref)
            st_ref[...] = jnp.zeros_like(st_ref)

    x = x_ref[...]
    xb = x.astype(BF16)
    ncol = 512
    for n0 in range(0, ODD_COLS, ncol):
        n1 = min(n0 + ncol, ODD_COLS)
        z_ref[:, n0:n1] = jnp.dot(xb, win_ref[:, n0:n1], preferred_element_type=F32)

    sm = z_ref[:, O_SM:O_SM + LANES]
    ig_all = sm + mlp_ref[0:1, :]
    lf_all = _log_sigmoid(sm + mlp_ref[1:2, :])
    b_all = _chunk_cumsum(lf_all, chunk)
    lane = lax.broadcasted_iota(jnp.int32, (1, LANES), 1)
    gi_ref[...] = jnp.where(lane < ML_H, ig_all, b_all)
    sel = _selector(0)
    tril, _, _ = _tri_masks(chunk)

    def chunk_body(c, carry):
        r0 = pl.multiple_of(c * chunk, chunk)
        rows = pl.ds(r0, chunk)
        gblk = gi_ref[rows, :]
        grow = _rows_from_cols(sel, gblk)
        for h in range(ML_H):
            qh = z_ref[rows, O_MQ + h * ML_DK:O_MQ + (h + 1) * ML_DK] * (ML_DK ** -0.5)
            kh = z_ref[rows, O_MK + h * ML_DK:O_MK + (h + 1) * ML_DK]
            vh = z_ref[rows, O_MV + h * ML_DV:O_MV + (h + 1) * ML_DV]
            igc = gblk[:, h:h + 1]
            bc = gblk[:, ML_H + h:ML_H + h + 1]
            igr = grow[h:h + 1, :]
            br = grow[ML_H + h:ML_H + h + 1, :]
            m_old = m_ref[h:h + 1, 0:1]
            dm = jnp.where(tril, bc - br + igr, -jnp.inf)
            inter = bc + m_old
            mt = jnp.maximum(inter, jnp.max(dm, axis=-1, keepdims=True))
            wt = jnp.exp(dm - mt)
            sc = jnp.exp(inter - mt)
            kb = kh.astype(BF16)
            vb = vh.astype(BF16)
            qk = _dot_nt(qh, kb) * wt
            c_old = c_ref[h]
            n_old = n_ref[h:h + 1, :]
            num = sc * _dot(qh, c_old) + _dot(qk, vb)
            den = sc * jnp.sum(qh * n_old, axis=-1, keepdims=True) + jnp.sum(qk, axis=-1, keepdims=True)
            hh = num / jnp.maximum(jnp.abs(den), jnp.exp(-mt))
            ml = mt[chunk - 1:chunk, :]
            wk = jnp.exp(bc[chunk - 1:chunk, :] - bc + igc - ml)
            sl = jnp.exp(inter[chunk - 1:chunk, :] - ml)
            kw = kh * wk
            c_ref[h] = sl * c_old + _dot_tn(kw, vb)
            n_ref[h:h + 1, :] = sl * n_old + jnp.sum(kw, axis=0, keepdims=True)
            m_ref[h:h + 1, :] = jnp.broadcast_to(ml, (1, LANES))
            mu = jnp.mean(hh, axis=-1, keepdims=True)
            hc = hh - mu
            var = jnp.mean(hc * hc, axis=-1, keepdims=True)
            hn = hc * lax.rsqrt(var + EPS) * mng_ref[...]
            lo = slice(O_MO + h * ML_DV, O_MO + (h + 1) * ML_DV)
            lg = slice(O_MG + h * ML_DV, O_MG + (h + 1) * ML_DV)
            z_ref[rows, lo] = _sigmoid(z_ref[rows, lo]) * hn * _silu(z_ref[rows, lg])
        return carry

    lax.fori_loop(0, tb // chunk, chunk_body, 0)

    lbp = hlb_ref[...]
    e = jnp.exp(lbp - jnp.max(lbp, axis=0, keepdims=True))
    smx = e / jnp.sum(e, axis=0, keepdims=True)
    lb = smx[0:1, :] * 0.0
    for jj in range(1, layer + 1):
        lb = lb + smx[jj:jj + 1, :]
    hf = z_ref[:, O_HF:O_HF + D_MODEL]
    la = jnp.log(lb)
    lbv = jnp.log1p(-lb) + _log_sigmoid(hf)
    logf = jnp.maximum(la, lbv) + jnp.log1p(jnp.exp(-jnp.abs(la - lbv)))
    z_ref[:, O_MG:O_MG + D_MODEL] = _chunk_cumsum(logf, HG_BLOCK)
    z_ref[:, O_HF:O_HF + D_MODEL] = (1.0 - lb) * _sigmoid(-hf)
    z_ref[:, O_HQ:O_HQ + D_MODEL] = _silu(z_ref[:, O_HQ:O_HQ + D_MODEL]) * (HG_DK ** -0.5)
    ones_blk = jnp.ones((HG_DK, HG_DK), BF16)
    half = HG_BLOCK // 2
    rowi = lax.broadcasted_iota(jnp.int32, (HG_BLOCK, 1), 0)

    def block_body(bi, carry):
        r0 = pl.multiple_of(bi * HG_BLOCK, HG_BLOCK)
        rows = pl.ds(r0, HG_BLOCK)
        gb = z_ref[rows, O_MG:O_MG + D_MODEL]
        qb = z_ref[rows, O_HQ:O_HQ + D_MODEL]
        kb = z_ref[rows, O_HF:O_HF + D_MODEL]
        vb = z_ref[rows, O_HI:O_HI + D_MODEL]
        pieces = []
        for j in range(HG_BLOCK):
            lo = 0 if j < half else half
            dj = jnp.exp(jnp.where(rowi[lo:] >= j, gb[lo:] - gb[j:j + 1, :], -jnp.inf))
            pieces.append(qb[lo:] * dj * kb[j:j + 1, :])
        pcat = jnp.concatenate(pieces, axis=0).astype(BF16)
        eg = jnp.exp(gb)
        gl = gb[HG_BLOCK - 1:HG_BLOCK, :]
        egl = jnp.exp(gl)
        qe = qb * eg
        ke = kb * jnp.exp(gl - gb)
        for h in range(HG_H):
            ln = slice(h * HG_DK, (h + 1) * HG_DK)
            rsum = jnp.dot(pcat[:, ln], ones_blk, preferred_element_type=F32)
            vbh = vb[:, ln]
            st_old = st_ref[h]
            o = _dot_nt(qe[:, ln], st_old)
            off = 0
            for j in range(HG_BLOCK):
                if j < half:
                    o = o + rsum[off:off + HG_BLOCK] * vbh[j:j + 1, :]
                    off += HG_BLOCK
                else:
                    pad = jnp.concatenate(
                        [jnp.zeros((half, HG_DV), F32), rsum[off:off + half] * vbh[j:j + 1, :]], axis=0)
                    o = o + pad
                    off += half
            st_ref[h] = egl[:, ln] * st_old + _dot_tn(vbh, ke[:, ln])
            on = o * lax.rsqrt(jnp.mean(o * o, axis=-1, keepdims=True) + EPS) * hng_ref[...]
            lgt = slice(O_HGATE + h * HG_DV, O_HGATE + (h + 1) * HG_DV)
            z_ref[rows, lgt] = on * _silu(z_ref[rows, lgt])
        return carry

    lax.fori_loop(0, tb // HG_BLOCK, block_body, 0)

    y = jnp.dot(z_ref[:, 0:2 * D_MODEL].astype(BF16), wout_ref[...], preferred_element_type=F32)
    xo_ref[...] = _layer_norm_rows(ALPHA * x + y, lng_ref[...], lnb_ref[...])

    @pl.when(t == nt - 1)
    def _fin():
        co_ref[...] = c_ref[...]
        no_ref[...] = n_ref[...]
        mo_ref[...] = m_ref[...]
        for h in range(HG_H):
            hso_ref[h] = st_ref[h].T


def _const_spec(shape):
    nd = len(shape)
    return pl.BlockSpec(shape, lambda b, t: (0,) * nd, pipeline_mode=pl.Buffered(1))


def _batch_spec(shape):
    nd = len(shape)
    return pl.BlockSpec((None,) + tuple(shape), lambda b, t: (b,) + (0,) * nd)


def _time_block(T, tb_max):
    tb = min(T, tb_max)
    while T % tb:
        tb -= 1
    return tb


def _chunk_len(tb, max_blk):
    c = min(max_blk, tb)
    while tb % c:
        c -= 1
    return c


def _even_layer(x, params, state, tb_max):
    B, T, D = x.shape
    tb = _time_block(T, tb_max)
    chunk = _chunk_len(tb, CHUNK)
    has_state = state is not None
    (win, cwa, cba, wai, bai, lam, cwb, gdp, gng, wout, lng, lnb) = params
    consts = [win, cwa, cba, wai, bai, lam, cwb, gdp, gng, wout, lng, lnb]
    in_specs = [pl.BlockSpec((None, tb, D), lambda b, t: (b, t, 0))]
    in_specs += [_const_spec(c.shape) for c in consts]
    args = [x] + consts
    state_shapes = [(1, RG_W), (CONV_W - 1, RG_W), (GD_H, GD_DK, GD_DV), (CONV_W - 1, GD_CONV)]
    if has_state:
        h0, cva0, s0, cvb0 = state
        args += [h0.reshape(B, 1, RG_W), cva0, s0, cvb0]
        in_specs += [_batch_spec(s) for s in state_shapes]
    out_shape = [jax.ShapeDtypeStruct((B, T, D), x.dtype)]
    out_shape += [jax.ShapeDtypeStruct((B,) + s, x.dtype) for s in state_shapes]
    out_specs = [pl.BlockSpec((None, tb, D), lambda b, t: (b, t, 0))]
    out_specs += [_batch_spec(s) for s in state_shapes]
    scratch = [
        pltpu.VMEM((tb, EVEN_COLS), F32),
        pltpu.VMEM((CONV_PAD + tb, RG_W), F32),
        pltpu.VMEM((CONV_PAD + tb, GD_CONV), F32),
        pltpu.VMEM((1, RG_W), F32),
        pltpu.VMEM((GD_H, GD_DK, GD_DV), F32),
        pltpu.VMEM((tb, LANES), F32),
        pltpu.VMEM((tb, LANES), F32),
    ]
    outs = pl.pallas_call(
        functools.partial(_even_kernel, tb=tb, chunk=chunk, has_state=has_state),
        grid=(B, T // tb),
        in_specs=in_specs,
        out_specs=out_specs,
        out_shape=out_shape,
        scratch_shapes=scratch,
        compiler_params=pltpu.CompilerParams(
            dimension_semantics=("arbitrary", "arbitrary"), vmem_limit_bytes=VMEM_LIMIT_BYTES),
        name="even_layer_state" if has_state else "even_layer",
    )(*args)
    xo, ho, cvao, so, cvbo = outs
    return xo, ho.reshape(B, RG_W), cvao, so, cvbo


def _odd_layer(x, params, layer, state, tb_max):
    B, T, D = x.shape
    tb = _time_block(T, tb_max)
    chunk = _chunk_len(tb, CHUNK)
    has_state = state is not None
    (win, mlp, mng, hlb, hng, wout, lng, lnb) = params
    consts = [win, mlp, mng, hlb, hng, wout, lng, lnb]
    in_specs = [pl.BlockSpec((None, tb, D), lambda b, t: (b, t, 0))]
    in_specs += [_const_spec(c.shape) for c in consts]
    args = [x] + consts
    state_shapes = [(ML_H, ML_DK, ML_DV), (SUBLANES, ML_DK), (SUBLANES, LANES), (HG_H, HG_DK, HG_DV)]
    if has_state:
        c0, n0, m0, hs0 = state
        n0p = jnp.zeros((B, SUBLANES, ML_DK), F32).at[:, :ML_H, :].set(n0)
        m0p = jnp.zeros((B, SUBLANES, LANES), F32).at[:, :ML_H, :].set(
            jnp.broadcast_to(m0[:, :, None], (B, ML_H, LANES)))
        args += [c0, n0p, m0p, hs0]
        in_specs += [_batch_spec(s) for s in state_shapes]
    out_shape = [jax.ShapeDtypeStruct((B, T, D), x.dtype)]
    out_shape += [jax.ShapeDtypeStruct((B,) + s, x.dtype) for s in state_shapes]
    out_specs = [pl.BlockSpec((None, tb, D), lambda b, t: (b, t, 0))]
    out_specs += [_batch_spec(s) for s in state_shapes]
    scratch = [
        pltpu.VMEM((tb, ODD_COLS), F32),
        pltpu.VMEM((ML_H, ML_DK, ML_DV), F32),
        pltpu.VMEM((SUBLANES, ML_DK), F32),
        pltpu.VMEM((SUBLANES, LANES), F32),
        pltpu.VMEM((HG_H, HG_DV, HG_DK), F32),
        pltpu.VMEM((tb, LANES), F32),
        pltpu.VMEM((tb, LANES), F32),
    ]
    outs = pl.pallas_call(
        functools.partial(_odd_kernel, tb=tb, chunk=chunk, layer=layer, has_state=has_state),
        grid=(B, T // tb),
        in_specs=in_specs,
        out_specs=out_specs,
        out_shape=out_shape,
        scratch_shapes=scratch,
        compiler_params=pltpu.CompilerParams(
            dimension_semantics=("arbitrary", "arbitrary"), vmem_limit_bytes=VMEM_LIMIT_BYTES),
        name="odd_layer_state" if has_state else "odd_layer",
    )(*args)
    xo, co, no, mo, hso = outs
    return xo, co, no[:, :ML_H, :], mo[:, :ML_H, 0], hso


def _pad_lanes(cols):
    return jnp.pad(cols, ((0, 0), (0, LANES - cols.shape[1])))


def _even_params(j, w_in_even, rg_conv_w, rg_conv_b, rg_w_a, rg_b_a, rg_w_i, rg_b_i, rg_lambda,
                 gdn_conv_w, gdn_a_log, gdn_dt_bias, gdn_norm_g, w_out_even, ln_even_g, ln_even_b):
    w = w_in_even[j]
    xa, ga, qb, kb, vb = (w[:, i * 1024:(i + 1) * 1024] for i in range(5))
    small = w[:, 5120:5136]
    gb = w[:, 5136:6160]
    win = jnp.concatenate([xa, gb, ga, qb, kb, vb, _pad_lanes(small)], axis=1).astype(BF16)
    wai = jnp.concatenate([rg_w_a[j], rg_w_i[j]], axis=-1).astype(BF16)
    bai = jnp.stack([rg_b_a[j], rg_b_i[j]]).astype(F32)
    gdp = jnp.zeros((2, LANES), F32)
    gdp = gdp.at[0, GD_H:2 * GD_H].set(gdn_a_log[j]).at[1, GD_H:2 * GD_H].set(gdn_dt_bias[j])
    return (win, rg_conv_w[j], rg_conv_b[j][None, :], wai, bai, rg_lambda[j][None, :],
            gdn_conv_w[j], gdp, gdn_norm_g[j][None, :], w_out_even[j].astype(BF16),
            ln_even_g[j][None, :], ln_even_b[j][None, :])


def _odd_params(j, w_in_odd, ml_b_i, ml_b_f, ml_norm_g, hg_lb, hg_norm_g, w_out_odd, ln_odd_g, ln_odd_b):
    w = w_in_odd[j]
    mq, mk, mv = w[:, 0:512], w[:, 512:1024], w[:, 1024:2048]
    small = w[:, 2048:2056]
    mo, mg, hq, hf, hi, hgate = (w[:, 2056 + i * 1024:2056 + (i + 1) * 1024] for i in range(6))
    win = jnp.concatenate([mo, hgate, mg, mq, mk, mv, hq, hf, hi, _pad_lanes(small)], axis=1).astype(BF16)
    mlp = jnp.zeros((2, LANES), F32)
    mlp = mlp.at[0, 0:ML_H].set(ml_b_i[j]).at[1, ML_H:2 * ML_H].set(ml_b_f[j])
    return (win, mlp, ml_norm_g[j][None, :], hg_lb.astype(F32), hg_norm_g[j][None, :],
            w_out_odd[j].astype(BF16), ln_odd_g[j][None, :], ln_odd_b[j][None, :])


def _run(x, states, even_p, odd_p, tb_max):
    rg_h, rg_cv, gd_s, gd_cv, ml_c, ml_n, ml_m, hg_s = ([] for _ in range(8))
    for l in range(DEPTH):
        j = l // 2
        if l % 2 == 0:
            st = None if states is None else (states[0][j], states[1][j], states[2][j], states[3][j])
            x, h, cva, s, cvb = _even_layer(x, even_p[j], st, tb_max)
            rg_h.append(h); rg_cv.append(cva); gd_s.append(s); gd_cv.append(cvb)
        else:
            st = None if states is None else (states[4][j], states[5][j], states[6][j], states[7][j])
            x, c, n, m, s = _odd_layer(x, odd_p[j], j, st, tb_max)
            ml_c.append(c); ml_n.append(n); ml_m.append(m); hg_s.append(s)
    return (x, jnp.stack(rg_h), jnp.stack(rg_cv), jnp.stack(gd_s), jnp.stack(gd_cv),
            jnp.stack(ml_c), jnp.stack(ml_n), jnp.stack(ml_m), jnp.stack(hg_s))


def _forward(x_prompt, x_sample, states, even_w, odd_w, tb_max=TIME_BLOCK):
    n_even = even_w[0].shape[0]
    n_odd = odd_w[0].shape[0]
    even_p = [_even_params(j, *even_w) for j in range(n_even)]
    odd_p = [_odd_params(j, *odd_w) for j in range(n_odd)]
    p = _run(x_prompt, None, even_p, odd_p, tb_max)
    s = _run(x_sample, states, even_p, odd_p, tb_max)
    return (p[0], s[0]) + p[1:] + s[1:]


def kernel(x_prompt, x_sample, state_rglru_h, state_rglru_conv, state_gdn_S, state_gdn_conv, state_mlstm_C, state_mlstm_n, state_mlstm_m, state_hgrn_S, w_in_even, rg_conv_w, rg_conv_b, rg_w_a, rg_b_a, rg_w_i, rg_b_i, rg_lambda, gdn_conv_w, gdn_a_log, gdn_dt_bias, gdn_norm_g, w_out_even, ln_even_g, ln_even_b, w_in_odd, ml_b_i, ml_b_f, ml_norm_g, hg_lb, hg_norm_g, w_out_odd, ln_odd_g, ln_odd_b):
    states = (state_rglru_h, state_rglru_conv, state_gdn_S, state_gdn_conv,
              state_mlstm_C, state_mlstm_n, state_mlstm_m, state_hgrn_S)
    even_w = (w_in_even, rg_conv_w, rg_conv_b, rg_w_a, rg_b_a, rg_w_i, rg_b_i, rg_lambda,
              gdn_conv_w, gdn_a_log, gdn_dt_bias, gdn_norm_g, w_out_even, ln_even_g, ln_even_b)
    odd_w = (w_in_odd, ml_b_i, ml_b_f, ml_norm_g, hg_lb, hg_norm_g, w_out_odd, ln_odd_g, ln_odd_b)
    return _forward(x_prompt, x_sample, states, even_w, odd_w)
```

```python
import functools
import math

import jax
import jax.numpy as jnp
from jax import lax
from jax.experimental import pallas as pl
from jax.experimental.pallas import tpu as pltpu

F32 = jnp.float32
BF16 = jnp.bfloat16

D_MODEL = 1024
DEPTH = 4
CONV_W = 4
ALPHA = (2 * DEPTH) ** 0.25
EPS = 1e-5
RG_W = D_MODEL
RG_BLOCKS = 8
RG_BW = RG_W // RG_BLOCKS
RG_C = 8.0
GD_H = 8
GD_DK = D_MODEL // GD_H
GD_DV = D_MODEL // GD_H
GD_CONV = 3 * D_MODEL
ML_H = 4
ML_DK = D_MODEL // (2 * ML_H)
ML_DV = D_MODEL // ML_H
ML_QK = ML_H * ML_DK
HG_H = 8
HG_DK = D_MODEL // HG_H
HG_DV = D_MODEL // HG_H
HG_BLOCK = 16
CHUNK = 64

LANES = 128
SUBLANES = 8
CONV_PAD = SUBLANES
TIME_BLOCK = 256
VMEM_LIMIT_BYTES = 56 * 1024 * 1024

E_XA, E_GB, E_GA, E_Q, E_K, E_V, E_SM = 0, 1024, 2048, 3072, 4096, 5120, 6144
EVEN_COLS = E_SM + LANES
O_MO, O_HGATE, O_MG, O_MQ, O_MK, O_MV, O_HQ, O_HF, O_HI, O_SM = (
    0, 1024, 2048, 3072, 3584, 4096, 5120, 6144, 7168, 8192)
ODD_COLS = O_SM + LANES


def _dot(a, b):
    return jnp.dot(a.astype(BF16), b.astype(BF16), preferred_element_type=F32)


def _dot_nt(a, b):
    return lax.dot_general(a.astype(BF16), b.astype(BF16), (((1,), (1,)), ((), ())),
                           preferred_element_type=F32)


def _dot_tn(a, b):
    return lax.dot_general(a.astype(BF16), b.astype(BF16), (((0,), (0,)), ((), ())),
                           preferred_element_type=F32)


def _split2(x):
    hi = x.astype(BF16)
    lo = (x - hi.astype(F32)).astype(BF16)
    return hi, lo


def _dot_x3(a, b):
    a_hi, a_lo = _split2(a)
    b_hi, b_lo = _split2(b)
    acc = jnp.dot(a_hi, b_hi, preferred_element_type=F32)
    acc += jnp.dot(a_hi, b_lo, preferred_element_type=F32)
    acc += jnp.dot(a_lo, b_hi, preferred_element_type=F32)
    return acc


def _rows_from_cols(sel, m):
    hi = m.astype(BF16)
    r1 = m - hi.astype(F32)
    mid = r1.astype(BF16)
    lo = (r1 - mid.astype(F32)).astype(BF16)
    dn = (((1,), (1,)), ((), ()))
    out = lax.dot_general(sel, hi, dn, preferred_element_type=F32)
    out += lax.dot_general(sel, mid, dn, preferred_element_type=F32)
    out += lax.dot_general(sel, lo, dn, preferred_element_type=F32)
    return out


def _selector(lane0):
    r = lax.broadcasted_iota(jnp.int32, (SUBLANES, LANES), 0)
    c = lax.broadcasted_iota(jnp.int32, (SUBLANES, LANES), 1)
    return (c == r + lane0).astype(BF16)


def _sigmoid(x):
    return jax.nn.sigmoid(x)


def _silu(x):
    return x * jax.nn.sigmoid(x)


def _softplus(x):
    return jnp.maximum(x, 0.0) + jnp.log1p(jnp.exp(-jnp.abs(x)))


def _log_sigmoid(x):
    return jnp.minimum(x, 0.0) - jnp.log1p(jnp.exp(-jnp.abs(x)))


def _chunk_cumsum(x, chunk):
    n = x.shape[0]
    pos = lax.broadcasted_iota(jnp.int32, (n, 1), 0) % chunk
    d = 1
    while d < chunk:
        x = x + jnp.where(pos >= d, pltpu.roll(x, d, 0), 0.0)
        d *= 2
    return x


def _causal_conv(ext_ref, u, w_ref, tb):
    ext_ref[CONV_PAD:CONV_PAD + tb, :] = u
    base = CONV_PAD - (CONV_W - 1)
    y = ext_ref[base:base + tb, :] * w_ref[0:1, :]
    for j in range(1, CONV_W):
        y = y + ext_ref[base + j:base + j + tb, :] * w_ref[j:j + 1, :]
    ext_ref[base:CONV_PAD, :] = ext_ref[base + tb:CONV_PAD + tb, :]
    return y


def _layer_norm_rows(x, g, b):
    mu = jnp.mean(x, axis=-1, keepdims=True)
    xc = x - mu
    var = jnp.mean(xc * xc, axis=-1, keepdims=True)
    return xc * lax.rsqrt(var + EPS) * g + b


def _tri_masks(c):
    r = lax.broadcasted_iota(jnp.int32, (c, c), 0)
    s = lax.broadcasted_iota(jnp.int32, (c, c), 1)
    return s <= r, s < r, (s == r).astype(F32)


def _unit_lower_inverse(a, eye):
    c = a.shape[0]
    p = eye - a
    q = _dot_x3(a, a)
    span = 2
    while span < c:
        p = p + _dot_x3(p, q)
        span *= 2
        if span < c:
            q = _dot_x3(q, q)
    return p


def _even_kernel(*refs, tb, chunk, has_state):
    (x_ref, win_ref, cwa_ref, cba_ref, wai_ref, bai_ref, lam_ref, cwb_ref, gdp_ref, gng_ref,
     wout_ref, lng_ref, lnb_ref) = refs[:13]
    pos = 13
    if has_state:
        h0_ref, cva0_ref, s0_ref, cvb0_ref = refs[pos:pos + 4]
        pos += 4
    xo_ref, ho_ref, cvao_ref, so_ref, cvbo_ref = refs[pos:pos + 5]
    pos += 5
    (z_ref, y_ref, exta_ref, extb_ref, h_ref, s_ref, g_ref, beta_ref,
     p_ref, q_ref, qkm_ref) = refs[pos:pos + 11]

    t = pl.program_id(1)
    nt = pl.num_programs(1)
    base = CONV_PAD - (CONV_W - 1)

    @pl.when(t == 0)
    def _init():
        exta_ref[0:CONV_PAD, :] = jnp.zeros((CONV_PAD, RG_W), F32)
        extb_ref[0:CONV_PAD, :] = jnp.zeros((CONV_PAD, GD_CONV), F32)
        if has_state:
            h_ref[...] = h0_ref[...]
            exta_ref[base:CONV_PAD, :] = cva0_ref[...]
            extb_ref[base:CONV_PAD, :] = cvb0_ref[...]
            s_ref[...] = s0_ref[...]
        else:
            h_ref[...] = jnp.zeros_like(h_ref)
            s_ref[...] = jnp.zeros_like(s_ref)

    x = x_ref[...]
    xb = x.astype(BF16)
    ncol = 512
    for n0 in range(0, EVEN_COLS, ncol):
        n1 = min(n0 + ncol, EVEN_COLS)
        z_ref[:, n0:n1] = jnp.dot(xb, win_ref[:, n0:n1], preferred_element_type=F32)

    xa = _causal_conv(exta_ref, z_ref[:, E_XA:E_XA + RG_W], cwa_ref, tb) + cba_ref[...]
    gates = []
    for g in range(RG_BLOCKS):
        gates.append(_dot(xa[:, g * RG_BW:(g + 1) * RG_BW], wai_ref[g]))
    pre_r = jnp.concatenate([gt[:, :RG_BW] for gt in gates], axis=-1) + bai_ref[0:1, :]
    pre_i = jnp.concatenate([gt[:, RG_BW:] for gt in gates], axis=-1) + bai_ref[1:2, :]
    r = _sigmoid(pre_r)
    i = _sigmoid(pre_i)
    log_a = (-RG_C) * r * _softplus(-lam_ref[...])
    a = jnp.exp(log_a)
    th = jnp.tanh(log_a)
    u = jnp.sqrt((-2.0) * th / (1.0 - th)) * (i * xa)
    row = lax.broadcasted_iota(jnp.int32, (tb, 1), 0)
    d = 1
    while d < tb:
        m = row >= d
        u = jnp.where(m, a * pltpu.roll(u, d, 0) + u, u)
        a = jnp.where(m, a * pltpu.roll(a, d, 0), a)
        d *= 2
    hs = u + a * h_ref[...]
    h_ref[...] = hs[tb - 1:tb, :]
    y_ref[:, 0:RG_W] = hs * _silu(z_ref[:, E_GA:E_GA + RG_W])

    qkv = _silu(_causal_conv(extb_ref, z_ref[:, E_Q:E_Q + GD_CONV], cwb_ref, tb))
    z_ref[:, E_Q:E_Q + GD_CONV] = qkv
    sm = z_ref[:, E_SM:E_SM + LANES]
    beta_ref[...] = _sigmoid(sm)
    g_all = -jnp.exp(gdp_ref[0:1, :]) * _softplus(sm + gdp_ref[1:2, :])
    g_ref[...] = _chunk_cumsum(g_all, chunk)
    for h in range(GD_H):
        lq = slice(E_Q + h * GD_DK, E_Q + (h + 1) * GD_DK)
        lk = slice(E_K + h * GD_DK, E_K + (h + 1) * GD_DK)
        qh = z_ref[:, lq]
        z_ref[:, lq] = qh * (lax.rsqrt(jnp.sum(qh * qh, axis=-1, keepdims=True) + 1e-6) * (GD_DK ** -0.5))
        kh = z_ref[:, lk]
        z_ref[:, lk] = kh * lax.rsqrt(jnp.sum(kh * kh, axis=-1, keepdims=True) + 1e-6)

    gw = 2 * LANES
    hg = gw // chunk
    ng = GD_H // hg
    nc = tb // chunk
    lc = chunk.bit_length() - 1
    assert 1 << lc == chunk and hg * chunk == gw and ng * hg == GD_H

    def _iota(shape, dim):
        return lax.broadcasted_iota(jnp.int32, shape, dim)

    ri = _iota((chunk, gw), 0)
    cj = jnp.bitwise_and(_iota((chunk, gw), 1), chunk - 1)
    tril_cat = cj <= ri
    strict_cat = cj < ri
    eye_cat = (cj == ri).astype(F32)
    bdmask = (lax.shift_right_logical(_iota((gw, gw), 0), lc)
              == lax.shift_right_logical(_iota((gw, gw), 1), lc))
    kw = hg * GD_DK
    bdmask_k = (lax.shift_right_logical(_iota((gw, kw), 0), lc)
                == lax.shift_right_logical(_iota((gw, kw), 1), 7))
    ex = (_iota((LANES, 2 * GD_H * chunk), 0)
          == lax.shift_right_logical(_iota((LANES, 2 * GD_H * chunk), 1), lc)).astype(BF16)
    lane = _iota((1, LANES), 1)
    sel = _selector(GD_H)

    def block_diag(xcat):
        return jnp.where(bdmask, jnp.concatenate([xcat] * hg, axis=0), 0.0).astype(BF16)

    for c in range(nc):
        rows = slice(c * chunk, (c + 1) * chunk)
        gblk = g_ref[rows, :]
        bblk = beta_ref[rows, :]
        grow = _rows_from_cols(sel, gblk)
        gr_all = jnp.concatenate([grow[h:h + 1, :] for h in range(GD_H)], axis=-1)
        m = jnp.where(lane < GD_H, bblk, gblk)
        hi = m.astype(BF16)
        r1 = m - hi.astype(F32)
        mid = r1.astype(BF16)
        lo = (r1 - mid.astype(F32)).astype(BF16)
        ex3 = (jnp.dot(hi, ex, preferred_element_type=F32) + jnp.dot(mid, ex, preferred_element_type=F32)
               + jnp.dot(lo, ex, preferred_element_type=F32))
        for g in range(ng):
            item = c * ng + g
            bcat = ex3[:, g * gw:(g + 1) * gw]
            gcat = ex3[:, GD_H * chunk + g * gw:GD_H * chunk + (g + 1) * gw]
            dec = jnp.exp(jnp.where(tril_cat, gcat - gr_all[:, g * gw:(g + 1) * gw], -jnp.inf))
            k_g = z_ref[rows, E_K + g * kw:E_K + (g + 1) * kw]
            q_g = z_ref[rows, E_Q + g * kw:E_Q + (g + 1) * kw]
            kq = jnp.concatenate([k_g, q_g], axis=0)
            kbd = jnp.where(bdmask_k, jnp.concatenate([k_g] * hg, axis=0), 0.0)
            res = _dot_nt(kq, kbd)
            amat = jnp.where(strict_cat, bcat * res[:chunk] * dec, 0.0)
            qkm_ref[item] = res[chunk:] * dec
            p_ref[item] = eye_cat - amat
            q_ref[item] = amat

    for item in range(nc * ng):
        amat = q_ref[item]
        q_ref[item] = _dot(amat, block_diag(amat))
    span = 2
    while span < chunk:
        last = 2 * span >= chunk
        for item in range(nc * ng):
            p = p_ref[item]
            q = q_ref[item]
            qbd = block_diag(q)
            if last:
                p_ref[item] = p + _dot(p, qbd)
            else:
                pq = _dot(jnp.concatenate([p, q], axis=0), qbd)
                p_ref[item] = p + pq[:chunk]
                q_ref[item] = pq[chunk:]
        span *= 2

    for c in range(nc):
        rows = slice(c * chunk, (c + 1) * chunk)
        gblk = g_ref[rows, :]
        bblk = beta_ref[rows, :]
        first = []
        for h in range(GD_H):
            lq = slice(E_Q + h * GD_DK, E_Q + (h + 1) * GD_DK)
            lk = slice(E_K + h * GD_DK, E_K + (h + 1) * GD_DK)
            gc = gblk[:, GD_H + h:GD_H + h + 1]
            bc = bblk[:, h:h + 1]
            eg = jnp.exp(gc)
            kh = z_ref[rows, lk]
            s_old = s_ref[h]
            r1 = _dot(jnp.concatenate([kh * (bc * eg), z_ref[rows, lq] * eg], axis=0), s_old)
            first.append((gc, bc, kh, s_old, r1))
        second = []
        for h in range(GD_H):
            gc, bc, kh, s_old, r1 = first[h]
            item = c * ng + h // hg
            li = (h % hg) * chunk
            lv = slice(E_V + h * GD_DV, E_V + (h + 1) * GD_DV)
            dlt = _dot(p_ref[item, :, li:li + chunk], z_ref[rows, lv] * bc - r1[:chunk])
            second.append(dlt)
        for h in range(GD_H):
            gc, bc, kh, s_old, r1 = first[h]
            dlt = second[h]
            item = c * ng + h // hg
            li = (h % hg) * chunk
            o = r1[chunk:] + _dot(qkm_ref[item, :, li:li + chunk], dlt)
            gl = gc[chunk - 1:chunk, :]
            s_ref[h] = jnp.exp(gl) * s_old + _dot_tn(kh * jnp.exp(gl - gc), dlt)
            on = o * lax.rsqrt(jnp.mean(o * o, axis=-1, keepdims=True) + EPS) * gng_ref[...]
            lg = slice(E_GB + h * GD_DV, E_GB + (h + 1) * GD_DV)
            ly = slice(RG_W + h * GD_DV, RG_W + (h + 1) * GD_DV)
            y_ref[rows, ly] = on * _silu(z_ref[rows, lg])

    y = jnp.dot(y_ref[...].astype(BF16), wout_ref[...], preferred_element_type=F32)
    xo_ref[...] = _layer_norm_rows(ALPHA * x + y, lng_ref[...], lnb_ref[...])

    @pl.when(t == nt - 1)
    def _fin():
        ho_ref[...] = h_ref[...]
        cvao_ref[...] = exta_ref[base:CONV_PAD, :]
        cvbo_ref[...] = extb_ref[base:CONV_PAD, :]
        so_ref[...] = s_ref[...]


def _odd_kernel(*refs, tb, chunk, layer, has_state):
    (x_ref, win_ref, mlp_ref, mng_ref, hlb_ref, hng_ref, wout_ref, lng_ref, lnb_ref) = refs[:9]
    pos = 9
    if has_state:
        c0_ref, n0_ref, m0_ref, hs0_ref = refs[pos:pos + 4]
        pos += 4
    xo_ref, co_ref, no_ref, mo_ref, hso_ref = refs[pos:pos + 5]
    pos += 5
    z_ref, y_ref, c_ref, n_ref, m_ref, st_ref, gi_ref = refs[pos:pos + 7]

    t = pl.program_id(1)
    nt = pl.num_programs(1)

    @pl.when(t == 0)
    def _init():
        if has_state:
            c_ref[...] = c0_ref[...]
            n_ref[...] = n0_ref[...]
            m_ref[...] = m0_ref[...]
            for h in range(HG_H):
                st_ref[h] = hs0_ref[h].T
        else:
            c_ref[...] = jnp.zeros_like(c_ref)
            n_ref[...] = jnp.zeros_like(n_ref)
            m_ref[...] = jnp.zeros_like(m_ref)
            st_ref[...] = jnp.zeros_like(st_ref)

    x = x_ref[...]
    xb = x.astype(BF16)
    ncol = 512
    for n0 in range(0, ODD_COLS, ncol):
        n1 = min(n0 + ncol, ODD_COLS)
        z_ref[:, n0:n1] = jnp.dot(xb, win_ref[:, n0:n1], preferred_element_type=F32)

    sm = z_ref[:, O_SM:O_SM + LANES]
    ig_all = sm + mlp_ref[0:1, :]
    lf_all = _log_sigmoid(sm + mlp_ref[1:2, :])
    b_all = _chunk_cumsum(lf_all, chunk)
    lane = lax.broadcasted_iota(jnp.int32, (1, LANES), 1)
    gi_ref[...] = jnp.where(lane < ML_H, ig_all, b_all)
    sel = _selector(0)
    tril, _, _ = _tri_masks(chunk)

    def chunk_body(c, carry):
        r0 = pl.multiple_of(c * chunk, chunk)
        rows = pl.ds(r0, chunk)
        gblk = gi_ref[rows, :]
        grow = _rows_from_cols(sel, gblk)
        for h in range(ML_H):
            qh = z_ref[rows, O_MQ + h * ML_DK:O_MQ + (h + 1) * ML_DK] * (ML_DK ** -0.5)
            kh = z_ref[rows, O_MK + h * ML_DK:O_MK + (h + 1) * ML_DK]
            vh = z_ref[rows, O_MV + h * ML_DV:O_MV + (h + 1) * ML_DV]
            igc = gblk[:, h:h + 1]
            bc = gblk[:, ML_H + h:ML_H + h + 1]
            igr = grow[h:h + 1, :]
            br = grow[ML_H + h:ML_H + h + 1, :]
            m_old = m_ref[h:h + 1, 0:1]
            dm = jnp.where(tril, bc - br + igr, -jnp.inf)
            inter = bc + m_old
            mt = jnp.maximum(inter, jnp.max(dm, axis=-1, keepdims=True))
            wt = jnp.exp(dm - mt)
            sc = jnp.exp(inter - mt)
            kb = kh.astype(BF16)
            vb = vh.astype(BF16)
            qk = _dot_nt(qh, kb) * wt
            c_old = c_ref[h]
            n_old = n_ref[h:h + 1, :]
            num = sc * _dot(qh, c_old) + _dot(qk, vb)
            den = sc * jnp.sum(qh * n_old, axis=-1, keepdims=True) + jnp.sum(qk, axis=-1, keepdims=True)
            hh = num / jnp.maximum(jnp.abs(den), jnp.exp(-mt))
            ml = mt[chunk - 1:chunk, :]
            wk = jnp.exp(bc[chunk - 1:chunk, :] - bc + igc - ml)
            sl = jnp.exp(inter[chunk - 1:chunk, :] - ml)
            kw = kh * wk
            c_ref[h] = sl * c_old + _dot_tn(kw, vb)
            n_ref[h:h + 1, :] = sl * n_old + jnp.sum(kw, axis=0, keepdims=True)
            m_ref[h:h + 1, :] = jnp.broadcast_to(ml, (1, LANES))
            mu = jnp.mean(hh, axis=-1, keepdims=True)
            hc = hh - mu
            var = jnp.mean(hc * hc, axis=-1, keepdims=True)
            hn = hc * lax.rsqrt(var + EPS) * mng_ref[...]
            lo = slice(O_MO + h * ML_DV, O_MO + (h + 1) * ML_DV)
            lg = slice(O_MG + h * ML_DV, O_MG + (h + 1) * ML_DV)
            y_ref[rows, h * ML_DV:(h + 1) * ML_DV] = _sigmoid(z_ref[rows, lo]) * hn * _silu(z_ref[rows, lg])
        return carry

    lax.fori_loop(0, tb // chunk, chunk_body, 0)

    lbp = hlb_ref[...]
    e = jnp.exp(lbp - jnp.max(lbp, axis=0, keepdims=True))
    smx = e / jnp.sum(e, axis=0, keepdims=True)
    lb = smx[0:1, :] * 0.0
    for jj in range(1, layer + 1):
        lb = lb + smx[jj:jj + 1, :]
    hf = z_ref[:, O_HF:O_HF + D_MODEL]
    la = jnp.log(lb)
    lbv = jnp.log1p(-lb) + _log_sigmoid(hf)
    logf = jnp.maximum(la, lbv) + jnp.log1p(jnp.exp(-jnp.abs(la - lbv)))
    z_ref[:, O_MG:O_MG + D_MODEL] = _chunk_cumsum(logf, HG_BLOCK)
    z_ref[:, O_HF:O_HF + D_MODEL] = (1.0 - lb) * _sigmoid(-hf)
    z_ref[:, O_HQ:O_HQ + D_MODEL] = _silu(z_ref[:, O_HQ:O_HQ + D_MODEL]) * (HG_DK ** -0.5)
    ones_blk = jnp.ones((HG_DK, HG_DK), BF16)
    half = HG_BLOCK // 2
    rowi = lax.broadcasted_iota(jnp.int32, (HG_BLOCK, 1), 0)

    def block_body(bi, carry):
        r0 = pl.multiple_of(bi * HG_BLOCK, HG_BLOCK)
        rows = pl.ds(r0, HG_BLOCK)
        gb = z_ref[rows, O_MG:O_MG + D_MODEL]
        qb = z_ref[rows, O_HQ:O_HQ + D_MODEL]
        kb = z_ref[rows, O_HF:O_HF + D_MODEL]
        vb = z_ref[rows, O_HI:O_HI + D_MODEL]
        pieces = []
        for j in range(HG_BLOCK):
            lo = 0 if j < half else half
            dj = jnp.exp(jnp.where(rowi[lo:] >= j, gb[lo:] - gb[j:j + 1, :], -jnp.inf))
            pieces.append(qb[lo:] * dj * kb[j:j + 1, :])
        pcat = jnp.concatenate(pieces, axis=0).astype(BF16)
        eg = jnp.exp(gb)
        gl = gb[HG_BLOCK - 1:HG_BLOCK, :]
        egl = jnp.exp(gl)
        qe = qb * eg
        ke = kb * jnp.exp(gl - gb)
        for h in range(HG_H):
            ln = slice(h * HG_DK, (h + 1) * HG_DK)
            rsum = jnp.dot(pcat[:, ln], ones_blk, preferred_element_type=F32)
            vbh = vb[:, ln]
            st_old = st_ref[h]
            o = _dot_nt(qe[:, ln], st_old)
            off = 0
            for j in range(HG_BLOCK):
                if j < half:
                    o = o + rsum[off:off + HG_BLOCK] * vbh[j:j + 1, :]
                    off += HG_BLOCK
                else:
                    pad = jnp.concatenate(
                        [jnp.zeros((half, HG_DV), F32), rsum[off:off + half] * vbh[j:j + 1, :]], axis=0)
                    o = o + pad
                    off += half
            st_ref[h] = egl[:, ln] * st_old + _dot_tn(vbh, ke[:, ln])
            on = o * lax.rsqrt(jnp.mean(o * o, axis=-1, keepdims=True) + EPS) * hng_ref[...]
            lgt = slice(O_HGATE + h * HG_DV, O_HGATE + (h + 1) * HG_DV)
            y_ref[rows, D_MODEL + h * HG_DV:D_MODEL + (h + 1) * HG_DV] = on * _silu(z_ref[rows, lgt])
        return carry

    lax.fori_loop(0, tb // HG_BLOCK, block_body, 0)

    y = jnp.dot(y_ref[...].astype(BF16), wout_ref[...], preferred_element_type=F32)
    xo_ref[...] = _layer_norm_rows(ALPHA * x + y, lng_ref[...], lnb_ref[...])

    @pl.when(t == nt - 1)
    def _fin():
        co_ref[...] = c_ref[...]
        no_ref[...] = n_ref[...]
        mo_ref[...] = m_ref[...]
        for h in range(HG_H):
            hso_ref[h] = st_ref[h].T


def _const_spec(shape):
    nd = len(shape)
    return pl.BlockSpec(shape, lambda b, t: (0,) * nd, pipeline_mode=pl.Buffered(1))


def _batch_spec(shape):
    nd = len(shape)
    return pl.BlockSpec((None,) + tuple(shape), lambda b, t: (b,) + (0,) * nd)


def _time_block(T, tb_max):
    tb = min(T, tb_max)
    while T % tb:
        tb -= 1
    return tb


def _chunk_len(tb, max_blk):
    c = min(max_blk, tb)
    while tb % c:
        c -= 1
    return c


def _even_layer(x, params, state, tb_max):
    B, T, D = x.shape
    tb = _time_block(T, tb_max)
    chunk = _chunk_len(tb, CHUNK)
    has_state = state is not None
    (win, cwa, cba, wai, bai, lam, cwb, gdp, gng, wout, lng, lnb) = params
    consts = [win, cwa, cba, wai, bai, lam, cwb, gdp, gng, wout, lng, lnb]
    in_specs = [pl.BlockSpec((None, tb, D), lambda b, t: (b, t, 0))]
    in_specs += [_const_spec(c.shape) for c in consts]
    args = [x] + consts
    state_shapes = [(1, RG_W), (CONV_W - 1, RG_W), (GD_H, GD_DK, GD_DV), (CONV_W - 1, GD_CONV)]
    if has_state:
        h0, cva0, s0, cvb0 = state
        args += [h0.reshape(B, 1, RG_W), cva0, s0, cvb0]
        in_specs += [_batch_spec(s) for s in state_shapes]
    out_shape = [jax.ShapeDtypeStruct((B, T, D), x.dtype)]
    out_shape += [jax.ShapeDtypeStruct((B,) + s, x.dtype) for s in state_shapes]
    out_specs = [pl.BlockSpec((None, tb, D), lambda b, t: (b, t, 0))]
    out_specs += [_batch_spec(s) for s in state_shapes]
    scratch = [
        pltpu.VMEM((tb, EVEN_COLS), F32),
        pltpu.VMEM((tb, 2 * D_MODEL), F32),
        pltpu.VMEM((CONV_PAD + tb, RG_W), F32),
        pltpu.VMEM((CONV_PAD + tb, GD_CONV), F32),
        pltpu.VMEM((1, RG_W), F32),
        pltpu.VMEM((GD_H, GD_DK, GD_DV), F32),
        pltpu.VMEM((tb, LANES), F32),
        pltpu.VMEM((tb, LANES), F32),
    ]
    n_items = (tb // chunk) * (GD_H * chunk // (2 * LANES))
    scratch += [pltpu.VMEM((n_items, chunk, 2 * LANES), F32) for _ in range(3)]
    outs = pl.pallas_call(
        functools.partial(_even_kernel, tb=tb, chunk=chunk, has_state=has_state),
        grid=(B, T // tb),
        in_specs=in_specs,
        out_specs=out_specs,
        out_shape=out_shape,
        scratch_shapes=scratch,
        compiler_params=pltpu.CompilerParams(
            dimension_semantics=("arbitrary", "arbitrary"), vmem_limit_bytes=VMEM_LIMIT_BYTES),
        name="even_layer_state" if has_state else "even_layer",
    )(*args)
    xo, ho, cvao, so, cvbo = outs
    return xo, ho.reshape(B, RG_W), cvao, so, cvbo


def _odd_layer(x, params, layer, state, tb_max):
    B, T, D = x.shape
    tb = _time_block(T, tb_max)
    chunk = _chunk_len(tb, CHUNK)
    has_state = state is not None
    (win, mlp, mng, hlb, hng, wout, lng, lnb) = params
    consts = [win, mlp, mng, hlb, hng, wout, lng, lnb]
    in_specs = [pl.BlockSpec((None, tb, D), lambda b, t: (b, t, 0))]
    in_specs += [_const_spec(c.shape) for c in consts]
    args = [x] + consts
    state_shapes = [(ML_H, ML_DK, ML_DV), (SUBLANES, ML_DK), (SUBLANES, LANES), (HG_H, HG_DK, HG_DV)]
    if has_state:
        c0, n0, m0, hs0 = state
        n0p = jnp.zeros((B, SUBLANES, ML_DK), F32).at[:, :ML_H, :].set(n0)
        m0p = jnp.zeros((B, SUBLANES, LANES), F32).at[:, :ML_H, :].set(
            jnp.broadcast_to(m0[:, :, None], (B, ML_H, LANES)))
        args += [c0, n0p, m0p, hs0]
        in_specs += [_batch_spec(s) for s in state_shapes]
    out_shape = [jax.ShapeDtypeStruct((B, T, D), x.dtype)]
    out_shape += [jax.ShapeDtypeStruct((B,) + s, x.dtype) for s in state_shapes]
    out_specs = [pl.BlockSpec((None, tb, D), lambda b, t: (b, t, 0))]
    out_specs += [_batch_spec(s) for s in state_shapes]
    scratch = [
        pltpu.VMEM((tb, ODD_COLS), F32),
        pltpu.VMEM((tb, 2 * D_MODEL), F32),
        pltpu.VMEM((ML_H, ML_DK, ML_DV), F32),
        pltpu.VMEM((SUBLANES, ML_DK), F32),
        pltpu.VMEM((SUBLANES, LANES), F32),
        pltpu.VMEM((HG_H, HG_DV, HG_DK), F32),
        pltpu.VMEM((tb, LANES), F32),
    ]
    outs = pl.pallas_call(
        functools.partial(_odd_kernel, tb=tb, chunk=chunk, layer=layer, has_state=has_state),
        grid=(B, T // tb),
        in_specs=in_specs,
        out_specs=out_specs,
        out_shape=out_shape,
        scratch_shapes=scratch,
        compiler_params=pltpu.CompilerParams(
            dimension_semantics=("arbitrary", "arbitrary"), vmem_limit_bytes=VMEM_LIMIT_BYTES),
        name="odd_layer_state" if has_state else "odd_layer",
    )(*args)
    xo, co, no, mo, hso = outs
    return xo, co, no[:, :ML_H, :], mo[:, :ML_H, 0], hso


def _pad_lanes(cols):
    return jnp.pad(cols, ((0, 0), (0, LANES - cols.shape[1])))


def _even_params(j, w_in_even, rg_conv_w, rg_conv_b, rg_w_a, rg_b_a, rg_w_i, rg_b_i, rg_lambda,
                 gdn_conv_w, gdn_a_log, gdn_dt_bias, gdn_norm_g, w_out_even, ln_even_g, ln_even_b):
    w = w_in_even[j]
    xa, ga, qb, kb, vb = (w[:, i * 1024:(i + 1) * 1024] for i in range(5))
    small = w[:, 5120:5136]
    gb = w[:, 5136:6160]
    win = jnp.concatenate([xa, gb, ga, qb, kb, vb, _pad_lanes(small)], axis=1).astype(BF16)
    wai = jnp.concatenate([rg_w_a[j], rg_w_i[j]], axis=-1).astype(BF16)
    bai = jnp.stack([rg_b_a[j], rg_b_i[j]]).astype(F32)
    gdp = jnp.zeros((2, LANES), F32)
    gdp = gdp.at[0, GD_H:2 * GD_H].set(gdn_a_log[j]).at[1, GD_H:2 * GD_H].set(gdn_dt_bias[j])
    return (win, rg_conv_w[j], rg_conv_b[j][None, :], wai, bai, rg_lambda[j][None, :],
            gdn_conv_w[j], gdp, gdn_norm_g[j][None, :], w_out_even[j].astype(BF16),
            ln_even_g[j][None, :], ln_even_b[j][None, :])


def _odd_params(j, w_in_odd, ml_b_i, ml_b_f, ml_norm_g, hg_lb, hg_norm_g, w_out_odd, ln_odd_g, ln_odd_b):
    w = w_in_odd[j]
    mq, mk, mv = w[:, 0:512], w[:, 512:1024], w[:, 1024:2048]
    small = w[:, 2048:2056]
    mo, mg, hq, hf, hi, hgate = (w[:, 2056 + i * 1024:2056 + (i + 1) * 1024] for i in range(6))
    win = jnp.concatenate([mo, hgate, mg, mq, mk, mv, hq, hf, hi, _pad_lanes(small)], axis=1).astype(BF16)
    mlp = jnp.zeros((2, LANES), F32)
    mlp = mlp.at[0, 0:ML_H].set(ml_b_i[j]).at[1, ML_H:2 * ML_H].set(ml_b_f[j])
    return (win, mlp, ml_norm_g[j][None, :], hg_lb.astype(F32), hg_norm_g[j][None, :],
            w_out_odd[j].astype(BF16), ln_odd_g[j][None, :], ln_odd_b[j][None, :])


def _run(x, states, even_p, odd_p, tb_max):
    rg_h, rg_cv, gd_s, gd_cv, ml_c, ml_n, ml_m, hg_s = ([] for _ in range(8))
    for l in range(DEPTH):
        j = l // 2
        if l % 2 == 0:
            st = None if states is None else (states[0][j], states[1][j], states[2][j], states[3][j])
            x, h, cva, s, cvb = _even_layer(x, even_p[j], st, tb_max)
            rg_h.append(h); rg_cv.append(cva); gd_s.append(s); gd_cv.append(cvb)
        else:
            st = None if states is None else (states[4][j], states[5][j], states[6][j], states[7][j])
            x, c, n, m, s = _odd_layer(x, odd_p[j], j, st, tb_max)
            ml_c.append(c); ml_n.append(n); ml_m.append(m); hg_s.append(s)
    return (x, jnp.stack(rg_h), jnp.stack(rg_cv), jnp.stack(gd_s), jnp.stack(gd_cv),
            jnp.stack(ml_c), jnp.stack(ml_n), jnp.stack(ml_m), jnp.stack(hg_s))


def _forward(x_prompt, x_sample, states, even_w, odd_w, tb_max=TIME_BLOCK):
    n_even = even_w[0].shape[0]
    n_odd = odd_w[0].shape[0]
    even_p = [_even_params(j, *even_w) for j in range(n_even)]
    odd_p = [_odd_params(j, *odd_w) for j in range(n_odd)]
    p = _run(x_prompt, None, even_p, odd_p, tb_max)
    s = _run(x_sample, states, even_p, odd_p, tb_max)
    return (p[0], s[0]) + p[1:] + s[1:]


def kernel(x_prompt, x_sample, state_rglru_h, state_rglru_conv, state_gdn_S, state_gdn_conv, state_mlstm_C, state_mlstm_n, state_mlstm_m, state_hgrn_S, w_in_even, rg_conv_w, rg_conv_b, rg_w_a, rg_b_a, rg_w_i, rg_b_i, rg_lambda, gdn_conv_w, gdn_a_log, gdn_dt_bias, gdn_norm_g, w_out_even, ln_even_g, ln_even_b, w_in_odd, ml_b_i, ml_b_f, ml_norm_g, hg_lb, hg_norm_g, w_out_odd, ln_odd_g, ln_odd_b):
    states = (state_rglru_h, state_rglru_conv, state_gdn_S, state_gdn_conv,
              state_mlstm_C, state_mlstm_n, state_mlstm_m, state_hgrn_S)
    even_w = (w_in_even, rg_conv_w, rg_conv_b, rg_w_a, rg_b_a, rg_w_i, rg_b_i, rg_lambda,
              gdn_conv_w, gdn_a_log, gdn_dt_bias, gdn_norm_g, w_out_even, ln_even_g, ln_even_b)
    odd_w = (w_in_odd, ml_b_i, ml_b_f, ml_norm_g, hg_lb, hg_norm_g, w_out_odd, ln_odd_g, ln_odd_b)
    return _forward(x_prompt, x_sample, states, even_w, odd_w)
```

```python
import functools
import math

import jax
import jax.numpy as jnp
from jax import lax
from jax.experimental import pallas as pl
from jax.experimental.pallas import tpu as pltpu

F32 = jnp.float32
BF16 = jnp.bfloat16

D_MODEL = 1024
DEPTH = 4
CONV_W = 4
ALPHA = (2 * DEPTH) ** 0.25
EPS = 1e-5
RG_W = D_MODEL
RG_BLOCKS = 8
RG_BW = RG_W // RG_BLOCKS
RG_C = 8.0
GD_H = 8
GD_DK = D_MODEL // GD_H
GD_DV = D_MODEL // GD_H
GD_CONV = 3 * D_MODEL
ML_H = 4
ML_DK = D_MODEL // (2 * ML_H)
ML_DV = D_MODEL // ML_H
ML_QK = ML_H * ML_DK
HG_H = 8
HG_DK = D_MODEL // HG_H
HG_DV = D_MODEL // HG_H
HG_BLOCK = 16
CHUNK = 64
HG_UNROLL = 4
LOG2E = 1.4426950408889634

LANES = 128
SUBLANES = 8
CONV_PAD = SUBLANES
TIME_BLOCK = 256
VMEM_LIMIT_BYTES = 56 * 1024 * 1024

E_XA, E_GB, E_GA, E_Q, E_K, E_V, E_SM = 0, 1024, 2048, 3072, 4096, 5120, 6144
EVEN_COLS = E_SM + LANES
O_MO, O_HGATE, O_MG, O_MQ, O_MK, O_MV, O_HQ, O_HF, O_HI, O_SM = (
    0, 1024, 2048, 3072, 3584, 4096, 5120, 6144, 7168, 8192)
ODD_COLS = O_SM + LANES


def _dot(a, b):
    return jnp.dot(a.astype(BF16), b.astype(BF16), preferred_element_type=F32)


def _dot_nt(a, b):
    return lax.dot_general(a.astype(BF16), b.astype(BF16), (((1,), (1,)), ((), ())),
                           preferred_element_type=F32)


def _dot_tn(a, b):
    return lax.dot_general(a.astype(BF16), b.astype(BF16), (((0,), (0,)), ((), ())),
                           preferred_element_type=F32)


def _split2(x):
    hi = x.astype(BF16)
    lo = (x - hi.astype(F32)).astype(BF16)
    return hi, lo


def _dot_x3(a, b):
    a_hi, a_lo = _split2(a)
    b_hi, b_lo = _split2(b)
    acc = jnp.dot(a_hi, b_hi, preferred_element_type=F32)
    acc += jnp.dot(a_hi, b_lo, preferred_element_type=F32)
    acc += jnp.dot(a_lo, b_hi, preferred_element_type=F32)
    return acc


def _rows_from_cols(sel, m):
    hi = m.astype(BF16)
    r1 = m - hi.astype(F32)
    mid = r1.astype(BF16)
    lo = (r1 - mid.astype(F32)).astype(BF16)
    dn = (((1,), (1,)), ((), ()))
    out = lax.dot_general(sel, hi, dn, preferred_element_type=F32)
    out += lax.dot_general(sel, mid, dn, preferred_element_type=F32)
    out += lax.dot_general(sel, lo, dn, preferred_element_type=F32)
    return out


def _selector(lane0):
    r = lax.broadcasted_iota(jnp.int32, (SUBLANES, LANES), 0)
    c = lax.broadcasted_iota(jnp.int32, (SUBLANES, LANES), 1)
    return (c == r + lane0).astype(BF16)


def _sigmoid(x):
    return jax.nn.sigmoid(x)


def _silu(x):
    return x * jax.nn.sigmoid(x)


def _softplus(x):
    return jnp.maximum(x, 0.0) + jnp.log1p(jnp.exp(-jnp.abs(x)))


def _log_sigmoid(x):
    return jnp.minimum(x, 0.0) - jnp.log1p(jnp.exp(-jnp.abs(x)))


def _chunk_cumsum(x, chunk):
    n = x.shape[0]
    pos = lax.broadcasted_iota(jnp.int32, (n, 1), 0) % chunk
    d = 1
    while d < chunk:
        x = x + jnp.where(pos >= d, pltpu.roll(x, d, 0), 0.0)
        d *= 2
    return x


def _causal_conv(ext_ref, u, w_ref, tb):
    ext_ref[CONV_PAD:CONV_PAD + tb, :] = u
    base = CONV_PAD - (CONV_W - 1)
    ext = ext_ref[...]
    y = u * w_ref[CONV_W - 1:CONV_W, :]
    for s in range(1, CONV_W):
        y = y + pltpu.roll(ext, s, 0)[CONV_PAD:, :] * w_ref[CONV_W - 1 - s:CONV_W - s, :]
    ext_ref[base:CONV_PAD, :] = ext_ref[base + tb:CONV_PAD + tb, :]
    return y


def _layer_norm_rows(x, g, b):
    mu = jnp.mean(x, axis=-1, keepdims=True)
    xc = x - mu
    var = jnp.mean(xc * xc, axis=-1, keepdims=True)
    return xc * lax.rsqrt(var + EPS) * g + b


def _tri_masks(c):
    r = lax.broadcasted_iota(jnp.int32, (c, c), 0)
    s = lax.broadcasted_iota(jnp.int32, (c, c), 1)
    return s <= r, s < r, (s == r).astype(F32)


def _unit_lower_inverse(a, eye):
    c = a.shape[0]
    p = eye - a
    q = _dot_x3(a, a)
    span = 2
    while span < c:
        p = p + _dot_x3(p, q)
        span *= 2
        if span < c:
            q = _dot_x3(q, q)
    return p


def _even_kernel(*refs, tb, chunk, has_state):
    (x_ref, win_ref, cwa_ref, cba_ref, wai_ref, bai_ref, lam_ref, cwb_ref, gdp_ref, gng_ref,
     wout_ref, lng_ref, lnb_ref) = refs[:13]
    pos = 13
    if has_state:
        h0_ref, cva0_ref, s0_ref, cvb0_ref = refs[pos:pos + 4]
        pos += 4
    xo_ref, ho_ref, cvao_ref, so_ref, cvbo_ref = refs[pos:pos + 5]
    pos += 5
    (z_ref, y_ref, exta_ref, extb_ref, h_ref, s_ref, g_ref, beta_ref,
     p_ref, q_ref, qkm_ref) = refs[pos:pos + 11]

    t = pl.program_id(1)
    nt = pl.num_programs(1)
    base = CONV_PAD - (CONV_W - 1)

    @pl.when(t == 0)
    def _init():
        exta_ref[0:CONV_PAD, :] = jnp.zeros((CONV_PAD, RG_W), F32)
        extb_ref[0:CONV_PAD, :] = jnp.zeros((CONV_PAD, GD_CONV), F32)
        if has_state:
            h_ref[...] = h0_ref[...]
            exta_ref[base:CONV_PAD, :] = cva0_ref[...]
            extb_ref[base:CONV_PAD, :] = cvb0_ref[...]
            s_ref[...] = s0_ref[...]
        else:
            h_ref[...] = jnp.zeros_like(h_ref)
            s_ref[...] = jnp.zeros_like(s_ref)

    x = x_ref[...]
    xb = x.astype(BF16)
    ncol = 512
    for n0 in range(0, EVEN_COLS, ncol):
        n1 = min(n0 + ncol, EVEN_COLS)
        z_ref[:, n0:n1] = jnp.dot(xb, win_ref[:, n0:n1], preferred_element_type=F32)

    xa = _causal_conv(exta_ref, z_ref[:, E_XA:E_XA + RG_W], cwa_ref, tb) + cba_ref[...]
    gates = []
    for g in range(RG_BLOCKS):
        gates.append(_dot(xa[:, g * RG_BW:(g + 1) * RG_BW], wai_ref[g]))
    pre_r = jnp.concatenate([gt[:, :RG_BW] for gt in gates], axis=-1) + bai_ref[0:1, :]
    pre_i = jnp.concatenate([gt[:, RG_BW:] for gt in gates], axis=-1) + bai_ref[1:2, :]
    r = _sigmoid(pre_r)
    i = _sigmoid(pre_i)
    log_a = (-RG_C) * r * _softplus(-lam_ref[...])
    a = jnp.exp(log_a)
    th = jnp.tanh(log_a)
    u = jnp.sqrt((-2.0) * th / (1.0 - th)) * (i * xa)
    nt8 = tb // SUBLANES
    a3 = a.reshape(nt8, SUBLANES, RG_W)
    u3 = u.reshape(nt8, SUBLANES, RG_W)
    sub = lax.broadcasted_iota(jnp.int32, (1, SUBLANES, 1), 1)
    d = 1
    while d < SUBLANES:
        m = sub >= d
        u3 = a3 * jnp.where(m, pltpu.roll(u3, d, 1), 0.0) + u3
        a3 = a3 * jnp.where(m, pltpu.roll(a3, d, 1), 1.0)
        d *= 2
    hprev = h_ref[...]
    for r in range(nt8):
        ht = u3[r] + a3[r] * hprev
        hprev = ht[SUBLANES - 1:SUBLANES, :]
        rows8 = slice(r * SUBLANES, (r + 1) * SUBLANES)
        y_ref[rows8, 0:RG_W] = ht * _silu(z_ref[rows8, E_GA:E_GA + RG_W])
    h_ref[...] = hprev

    qkv = _silu(_causal_conv(extb_ref, z_ref[:, E_Q:E_Q + GD_CONV], cwb_ref, tb))
    z_ref[:, E_Q:E_Q + GD_CONV] = qkv
    sm = z_ref[:, E_SM:E_SM + LANES]
    beta_ref[...] = _sigmoid(sm)
    g_all = -jnp.exp(gdp_ref[0:1, :]) * _softplus(sm + gdp_ref[1:2, :])
    g_ref[...] = _chunk_cumsum(g_all, chunk)
    for h in range(GD_H):
        lq = slice(E_Q + h * GD_DK, E_Q + (h + 1) * GD_DK)
        lk = slice(E_K + h * GD_DK, E_K + (h + 1) * GD_DK)
        qh = z_ref[:, lq]
        z_ref[:, lq] = qh * (lax.rsqrt(jnp.sum(qh * qh, axis=-1, keepdims=True) + 1e-6) * (GD_DK ** -0.5))
        kh = z_ref[:, lk]
        z_ref[:, lk] = kh * lax.rsqrt(jnp.sum(kh * kh, axis=-1, keepdims=True) + 1e-6)

    gw = 2 * LANES
    hg = gw // chunk
    ng = GD_H // hg
    nc = tb // chunk
    lc = chunk.bit_length() - 1
    assert 1 << lc == chunk and hg * chunk == gw and ng * hg == GD_H

    def _iota(shape, dim):
        return lax.broadcasted_iota(jnp.int32, shape, dim)

    ri = _iota((chunk, gw), 0)
    cj = jnp.bitwise_and(_iota((chunk, gw), 1), chunk - 1)
    tril_cat = cj <= ri
    strict_cat = cj < ri
    eye_cat = (cj == ri).astype(F32)
    bdmask = (lax.shift_right_logical(_iota((gw, gw), 0), lc)
              == lax.shift_right_logical(_iota((gw, gw), 1), lc))
    kw = hg * GD_DK
    bdmask_k = (lax.shift_right_logical(_iota((gw, kw), 0), lc)
                == lax.shift_right_logical(_iota((gw, kw), 1), 7))
    ex = (_iota((LANES, 2 * GD_H * chunk), 0)
          == lax.shift_right_logical(_iota((LANES, 2 * GD_H * chunk), 1), lc)).astype(BF16)
    lane = _iota((1, LANES), 1)
    sel = _selector(GD_H)

    def block_diag(xcat):
        return jnp.where(bdmask, jnp.concatenate([xcat] * hg, axis=0), 0.0).astype(BF16)

    for c in range(nc):
        rows = slice(c * chunk, (c + 1) * chunk)
        gblk = g_ref[rows, :]
        bblk = beta_ref[rows, :]
        grow = _rows_from_cols(sel, gblk)
        gr_all = jnp.concatenate([grow[h:h + 1, :] for h in range(GD_H)], axis=-1)
        m = jnp.where(lane < GD_H, bblk, gblk)
        hi = m.astype(BF16)
        r1 = m - hi.astype(F32)
        mid = r1.astype(BF16)
        lo = (r1 - mid.astype(F32)).astype(BF16)
        ex3 = (jnp.dot(hi, ex, preferred_element_type=F32) + jnp.dot(mid, ex, preferred_element_type=F32)
               + jnp.dot(lo, ex, preferred_element_type=F32))
        for g in range(ng):
            item = c * ng + g
            bcat = ex3[:, g * gw:(g + 1) * gw]
            gcat = ex3[:, GD_H * chunk + g * gw:GD_H * chunk + (g + 1) * gw]
            dec = jnp.exp(jnp.where(tril_cat, gcat - gr_all[:, g * gw:(g + 1) * gw], -jnp.inf))
            k_g = z_ref[rows, E_K + g * kw:E_K + (g + 1) * kw]
            q_g = z_ref[rows, E_Q + g * kw:E_Q + (g + 1) * kw]
            kq = jnp.concatenate([k_g, q_g], axis=0)
            kbd = jnp.where(bdmask_k, jnp.concatenate([k_g] * hg, axis=0), 0.0)
            res = _dot_nt(kq, kbd)
            amat = jnp.where(strict_cat, bcat * res[:chunk] * dec, 0.0)
            qkm_ref[item] = res[chunk:] * dec
            p_ref[item] = eye_cat - amat
            q_ref[item] = amat

    for item in range(nc * ng):
        amat = q_ref[item]
        q_ref[item] = _dot(amat, block_diag(amat))
    span = 2
    while span < chunk:
        last = 2 * span >= chunk
        for item in range(nc * ng):
            p = p_ref[item]
            q = q_ref[item]
            qbd = block_diag(q)
            if last:
                p_ref[item] = p + _dot(p, qbd)
            else:
                pq = _dot(jnp.concatenate([p, q], axis=0), qbd)
                p_ref[item] = p + pq[:chunk]
                q_ref[item] = pq[chunk:]
        span *= 2

    for c in range(nc):
        rows = slice(c * chunk, (c + 1) * chunk)
        gblk = g_ref[rows, :]
        bblk = beta_ref[rows, :]
        first = []
        for h in range(GD_H):
            lq = slice(E_Q + h * GD_DK, E_Q + (h + 1) * GD_DK)
            lk = slice(E_K + h * GD_DK, E_K + (h + 1) * GD_DK)
            gc = gblk[:, GD_H + h:GD_H + h + 1]
            bc = bblk[:, h:h + 1]
            eg = jnp.exp(gc)
            kh = z_ref[rows, lk]
            s_old = s_ref[h]
            r1 = _dot(jnp.concatenate([kh * (bc * eg), z_ref[rows, lq] * eg], axis=0), s_old)
            first.append((gc, bc, kh, s_old, r1))
        second = []
        for h in range(GD_H):
            gc, bc, kh, s_old, r1 = first[h]
            item = c * ng + h // hg
            li = (h % hg) * chunk
            lv = slice(E_V + h * GD_DV, E_V + (h + 1) * GD_DV)
            dlt = _dot(p_ref[item, :, li:li + chunk], z_ref[rows, lv] * bc - r1[:chunk])
            second.append(dlt)
        for h in range(GD_H):
            gc, bc, kh, s_old, r1 = first[h]
            dlt = second[h]
            item = c * ng + h // hg
            li = (h % hg) * chunk
            o = r1[chunk:] + _dot(qkm_ref[item, :, li:li + chunk], dlt)
            gl = gc[chunk - 1:chunk, :]
            s_ref[h] = jnp.exp(gl) * s_old + _dot_tn(kh * jnp.exp(gl - gc), dlt)
            on = o * lax.rsqrt(jnp.mean(o * o, axis=-1, keepdims=True) + EPS) * gng_ref[...]
            lg = slice(E_GB + h * GD_DV, E_GB + (h + 1) * GD_DV)
            ly = slice(RG_W + h * GD_DV, RG_W + (h + 1) * GD_DV)
            y_ref[rows, ly] = on * _silu(z_ref[rows, lg])

    y = jnp.dot(y_ref[...].astype(BF16), wout_ref[...], preferred_element_type=F32)
    xo_ref[...] = _layer_norm_rows(ALPHA * x + y, lng_ref[...], lnb_ref[...])

    @pl.when(t == nt - 1)
    def _fin():
        ho_ref[...] = h_ref[...]
        cvao_ref[...] = exta_ref[base:CONV_PAD, :]
        cvbo_ref[...] = extb_ref[base:CONV_PAD, :]
        so_ref[...] = s_ref[...]


def _odd_kernel(*refs, tb, chunk, layer, has_state):
    (x_ref, win_ref, mlp_ref, mng_ref, hlb_ref, hng_ref, wout_ref, lng_ref, lnb_ref) = refs[:9]
    pos = 9
    if has_state:
        c0_ref, n0_ref, m0_ref, hs0_ref = refs[pos:pos + 4]
        pos += 4
    xo_ref, co_ref, no_ref, mo_ref, hso_ref = refs[pos:pos + 5]
    pos += 5
    z_ref, y_ref, c_ref, n_ref, m_ref, st_ref, gi_ref, a1_ref, kv_ref, loc_ref = refs[pos:pos + 10]

    t = pl.program_id(1)
    nt = pl.num_programs(1)
    hp = HG_H // 2

    @pl.when(t == 0)
    def _init():
        if has_state:
            c_ref[...] = c0_ref[...]
            n_ref[...] = n0_ref[...]
            m_ref[...] = m0_ref[...]
            for h in range(HG_H):
                st_ref[h // 2, :, (h % 2) * HG_DK:(h % 2 + 1) * HG_DK] = hs0_ref[h].T
        else:
            c_ref[...] = jnp.zeros_like(c_ref)
            n_ref[...] = jnp.zeros_like(n_ref)
            m_ref[...] = jnp.zeros_like(m_ref)
            st_ref[...] = jnp.zeros_like(st_ref)

    x = x_ref[...]
    xb = x.astype(BF16)
    ncol = 512
    for n0 in range(0, ODD_COLS, ncol):
        n1 = min(n0 + ncol, ODD_COLS)
        z_ref[:, n0:n1] = jnp.dot(xb, win_ref[:, n0:n1], preferred_element_type=F32)

    sm = z_ref[:, O_SM:O_SM + LANES]
    ig_all = sm + mlp_ref[0:1, :]
    lf_all = _log_sigmoid(sm + mlp_ref[1:2, :])
    b_all = _chunk_cumsum(lf_all, chunk)
    lane = lax.broadcasted_iota(jnp.int32, (1, LANES), 1)
    gi_ref[...] = jnp.where(lane < ML_H, ig_all, b_all)
    sel = _selector(0)
    tril, _, _ = _tri_masks(chunk)
    nc = tb // chunk

    for c in range(nc):
        rows = slice(c * chunk, (c + 1) * chunk)
        gblk = gi_ref[rows, :]
        grow = _rows_from_cols(sel, gblk)
        for h in range(ML_H):
            item = c * ML_H + h
            qh = z_ref[rows, O_MQ + h * ML_DK:O_MQ + (h + 1) * ML_DK] * (ML_DK ** -0.5)
            kh = z_ref[rows, O_MK + h * ML_DK:O_MK + (h + 1) * ML_DK]
            vh = z_ref[rows, O_MV + h * ML_DV:O_MV + (h + 1) * ML_DV].astype(BF16)
            igc = gblk[:, h:h + 1]
            bc = gblk[:, ML_H + h:ML_H + h + 1]
            dm = jnp.where(tril, bc - grow[ML_H + h:ML_H + h + 1, :] + grow[h:h + 1, :], -jnp.inf)
            md = jnp.max(dm, axis=-1, keepdims=True)
            qkl = _dot_nt(qh, kh) * jnp.exp(dm - md)
            a1_ref[item] = _dot(qkl, vh)
            kwl = kh * jnp.exp(bc[chunk - 1:chunk, :] - bc + igc - md[chunk - 1:chunk, :])
            kv_ref[item] = _dot_tn(kwl, vh)
            loc_ref[item, 0:chunk, 0:1] = md
            loc_ref[item, 0:chunk, 1:2] = jnp.sum(qkl, axis=-1, keepdims=True)
            loc_ref[item, chunk:chunk + 1, :] = jnp.sum(kwl, axis=0, keepdims=True)

    for c in range(nc):
        rows = slice(c * chunk, (c + 1) * chunk)
        gblk = gi_ref[rows, :]
        for h in range(ML_H):
            item = c * ML_H + h
            qh = z_ref[rows, O_MQ + h * ML_DK:O_MQ + (h + 1) * ML_DK] * (ML_DK ** -0.5)
            bc = gblk[:, ML_H + h:ML_H + h + 1]
            md = loc_ref[item, 0:chunk, 0:1]
            rs = loc_ref[item, 0:chunk, 1:2]
            kn = loc_ref[item, chunk:chunk + 1, :]
            m_old = m_ref[h:h + 1, 0:1]
            c_old = c_ref[h]
            n_old = n_ref[h:h + 1, :]
            inter = bc + m_old
            mt = jnp.maximum(inter, md)
            sc = jnp.exp(inter - mt)
            sd = jnp.exp(md - mt)
            num = sc * _dot(qh, c_old) + sd * a1_ref[item]
            den = sc * jnp.sum(qh * n_old, axis=-1, keepdims=True) + sd * rs
            hh = num / jnp.maximum(jnp.abs(den), jnp.exp(-mt))
            ml = mt[chunk - 1:chunk, :]
            sl = jnp.exp(inter[chunk - 1:chunk, :] - ml)
            sdl = sd[chunk - 1:chunk, :]
            c_ref[h] = sl * c_old + sdl * kv_ref[item]
            n_ref[h:h + 1, :] = sl * n_old + sdl * kn
            m_ref[h:h + 1, :] = jnp.broadcast_to(ml, (1, LANES))
            mu = jnp.mean(hh, axis=-1, keepdims=True)
            hc = hh - mu
            var = jnp.mean(hc * hc, axis=-1, keepdims=True)
            hn = hc * lax.rsqrt(var + EPS) * mng_ref[...]
            lo = slice(O_MO + h * ML_DV, O_MO + (h + 1) * ML_DV)
            lg = slice(O_MG + h * ML_DV, O_MG + (h + 1) * ML_DV)
            y_ref[rows, h * ML_DV:(h + 1) * ML_DV] = _sigmoid(z_ref[rows, lo]) * hn * _silu(z_ref[rows, lg])

    lbp = hlb_ref[...]
    e = jnp.exp(lbp - jnp.max(lbp, axis=0, keepdims=True))
    smx = e / jnp.sum(e, axis=0, keepdims=True)
    lb = smx[0:1, :] * 0.0
    for jj in range(1, layer + 1):
        lb = lb + smx[jj:jj + 1, :]
    hf = z_ref[:, O_HF:O_HF + D_MODEL]
    la = jnp.log(lb)
    lbv = jnp.log1p(-lb) + _log_sigmoid(hf)
    logf = jnp.maximum(la, lbv) + jnp.log1p(jnp.exp(-jnp.abs(la - lbv)))
    g2 = _chunk_cumsum(logf, HG_BLOCK) * LOG2E
    z_ref[:, O_MG:O_MG + D_MODEL] = g2
    z_ref[:, O_HF:O_HF + D_MODEL] = g2 - jnp.log((1.0 - lb) * _sigmoid(-hf)) * LOG2E
    z_ref[:, O_HQ:O_HQ + D_MODEL] = _silu(z_ref[:, O_HQ:O_HQ + D_MODEL]) * (HG_DK ** -0.5)
    half = HG_BLOCK // 2
    rowi = lax.broadcasted_iota(jnp.int32, (HG_BLOCK, 1), 0)
    kcat = HG_BLOCK * HG_DK
    sel_j = (lax.broadcasted_iota(jnp.int32, (HG_BLOCK, kcat), 0)
             == lax.shift_right_logical(lax.broadcasted_iota(jnp.int32, (HG_BLOCK, kcat), 1), 7)).astype(BF16)
    pair_mask = (lax.shift_right_logical(lax.broadcasted_iota(jnp.int32, (2 * HG_BLOCK, 2 * HG_DK), 0), 4)
                 == lax.shift_right_logical(lax.broadcasted_iota(jnp.int32, (2 * HG_BLOCK, 2 * HG_DK), 1), 7))
    zero_half = jnp.zeros((half, HG_DK), F32)
    zero_rows = jnp.zeros((LANES - HG_BLOCK, HG_H * HG_BLOCK), F32)

    ub = min(HG_UNROLL, tb // HG_BLOCK)

    def intra_body(bi, carry):
        r0 = pl.multiple_of(bi * (ub * HG_BLOCK), ub * HG_BLOCK)
        pcats = []
        for u in range(ub):
            rows = pl.ds(r0 + u * HG_BLOCK, HG_BLOCK)
            gb = z_ref[rows, O_MG:O_MG + D_MODEL]
            hb = z_ref[rows, O_HF:O_HF + D_MODEL]
            qb = z_ref[rows, O_HQ:O_HQ + D_MODEL]
            pieces = []
            for j in range(HG_BLOCK):
                lo = 0 if j < half else half
                pj = qb[lo:] * jnp.exp2(jnp.where(rowi[lo:] >= j, gb[lo:] - hb[j:j + 1, :], -jnp.inf))
                tiles = []
                for h in range(HG_H):
                    if lo:
                        tiles.append(zero_half)
                    tiles.append(pj[:, h * HG_DK:(h + 1) * HG_DK])
                pieces.append(jnp.concatenate(tiles, axis=0))
            pcats.append(jnp.concatenate(pieces, axis=1).astype(BF16))
        atts = []
        for u in range(ub):
            att_t = lax.dot_general(sel_j, pcats[u], (((1,), (1,)), ((), ())),
                                    preferred_element_type=F32)
            atts.append(jnp.concatenate([att_t, zero_rows], axis=0).T)
        for u in range(ub):
            rows = pl.ds(r0 + u * HG_BLOCK, HG_BLOCK)
            vb = z_ref[rows, O_HI:O_HI + D_MODEL]
            for h in range(HG_H):
                att_h = atts[u][h * HG_BLOCK:(h + 1) * HG_BLOCK, 0:HG_BLOCK]
                y_ref[rows, D_MODEL + h * HG_DV:D_MODEL + (h + 1) * HG_DV] = _dot(
                    att_h, vb[:, h * HG_DV:(h + 1) * HG_DV])
        return carry

    lax.fori_loop(0, tb // (ub * HG_BLOCK), intra_body, 0)

    def state_body(bi, carry):
        r0 = pl.multiple_of(bi * (ub * HG_BLOCK), ub * HG_BLOCK)
        for u in range(ub):
            rows = pl.ds(r0 + u * HG_BLOCK, HG_BLOCK)
            gb = z_ref[rows, O_MG:O_MG + D_MODEL]
            gl = gb[HG_BLOCK - 1:HG_BLOCK, :]
            egl = jnp.exp2(gl)
            qe = z_ref[rows, O_HQ:O_HQ + D_MODEL] * jnp.exp2(gb)
            ke = jnp.exp2(gl - z_ref[rows, O_HF:O_HF + D_MODEL])
            vb = z_ref[rows, O_HI:O_HI + D_MODEL]
            for h in range(HG_H):
                ly = slice(D_MODEL + h * HG_DV, D_MODEL + (h + 1) * HG_DV)
                st_old = st_ref[h // 2, :, (h % 2) * HG_DK:(h % 2 + 1) * HG_DK]
                o = y_ref[rows, ly] + _dot_nt(qe[:, h * HG_DK:(h + 1) * HG_DK], st_old)
                on = o * lax.rsqrt(jnp.mean(o * o, axis=-1, keepdims=True) + EPS) * hng_ref[...]
                lgt = slice(O_HGATE + h * HG_DV, O_HGATE + (h + 1) * HG_DV)
                y_ref[rows, ly] = on * _silu(z_ref[rows, lgt])
            for p in range(hp):
                lp = slice(2 * p * HG_DK, (2 * p + 2) * HG_DK)
                vst = jnp.concatenate([vb[:, 2 * p * HG_DV:(2 * p + 1) * HG_DV],
                                       vb[:, (2 * p + 1) * HG_DV:(2 * p + 2) * HG_DV]], axis=0)
                kep = ke[:, lp]
                rhs = jnp.where(pair_mask, jnp.concatenate([kep, kep], axis=0), 0.0)
                st_ref[p] = egl[:, lp] * st_ref[p] + _dot_tn(vst, rhs)
        return carry

    lax.fori_loop(0, tb // (ub * HG_BLOCK), state_body, 0)

    y = jnp.dot(y_ref[...].astype(BF16), wout_ref[...], preferred_element_type=F32)
    xo_ref[...] = _layer_norm_rows(ALPHA * x + y, lng_ref[...], lnb_ref[...])

    @pl.when(t == nt - 1)
    def _fin():
        co_ref[...] = c_ref[...]
        no_ref[...] = n_ref[...]
        mo_ref[...] = m_ref[...]
        for h in range(HG_H):
            hso_ref[h] = st_ref[h // 2, :, (h % 2) * HG_DK:(h % 2 + 1) * HG_DK].T


def _const_spec(shape):
    nd = len(shape)
    return pl.BlockSpec(shape, lambda b, t: (0,) * nd, pipeline_mode=pl.Buffered(1))


def _batch_spec(shape):
    nd = len(shape)
    return pl.BlockSpec((None,) + tuple(shape), lambda b, t: (b,) + (0,) * nd)


def _time_block(T, tb_max):
    tb = min(T, tb_max)
    while T % tb:
        tb -= 1
    return tb


def _chunk_len(tb, max_blk):
    c = min(max_blk, tb)
    while tb % c:
        c -= 1
    return c


def _even_layer(x, params, state, tb_max):
    B, T, D = x.shape
    tb = _time_block(T, tb_max)
    chunk = _chunk_len(tb, CHUNK)
    has_state = state is not None
    (win, cwa, cba, wai, bai, lam, cwb, gdp, gng, wout, lng, lnb) = params
    consts = [win, cwa, cba, wai, bai, lam, cwb, gdp, gng, wout, lng, lnb]
    in_specs = [pl.BlockSpec((None, tb, D), lambda b, t: (b, t, 0))]
    in_specs += [_const_spec(c.shape) for c in consts]
    args = [x] + consts
    state_shapes = [(1, RG_W), (CONV_W - 1, RG_W), (GD_H, GD_DK, GD_DV), (CONV_W - 1, GD_CONV)]
    if has_state:
        h0, cva0, s0, cvb0 = state
        args += [h0.reshape(B, 1, RG_W), cva0, s0, cvb0]
        in_specs += [_batch_spec(s) for s in state_shapes]
    out_shape = [jax.ShapeDtypeStruct((B, T, D), x.dtype)]
    out_shape += [jax.ShapeDtypeStruct((B,) + s, x.dtype) for s in state_shapes]
    out_specs = [pl.BlockSpec((None, tb, D), lambda b, t: (b, t, 0))]
    out_specs += [_batch_spec(s) for s in state_shapes]
    scratch = [
        pltpu.VMEM((tb, EVEN_COLS), F32),
        pltpu.VMEM((tb, 2 * D_MODEL), F32),
        pltpu.VMEM((CONV_PAD + tb, RG_W), F32),
        pltpu.VMEM((CONV_PAD + tb, GD_CONV), F32),
        pltpu.VMEM((1, RG_W), F32),
        pltpu.VMEM((GD_H, GD_DK, GD_DV), F32),
        pltpu.VMEM((tb, LANES), F32),
        pltpu.VMEM((tb, LANES), F32),
    ]
    n_items = (tb // chunk) * (GD_H * chunk // (2 * LANES))
    scratch += [pltpu.VMEM((n_items, chunk, 2 * LANES), F32) for _ in range(3)]
    outs = pl.pallas_call(
        functools.partial(_even_kernel, tb=tb, chunk=chunk, has_state=has_state),
        grid=(B, T // tb),
        in_specs=in_specs,
        out_specs=out_specs,
        out_shape=out_shape,
        scratch_shapes=scratch,
        compiler_params=pltpu.CompilerParams(
            dimension_semantics=("arbitrary", "arbitrary"), vmem_limit_bytes=VMEM_LIMIT_BYTES),
        name="even_layer_state" if has_state else "even_layer",
    )(*args)
    xo, ho, cvao, so, cvbo = outs
    return xo, ho.reshape(B, RG_W), cvao, so, cvbo


def _odd_layer(x, params, layer, state, tb_max):
    B, T, D = x.shape
    tb = _time_block(T, tb_max)
    chunk = _chunk_len(tb, CHUNK)
    has_state = state is not None
    (win, mlp, mng, hlb, hng, wout, lng, lnb) = params
    consts = [win, mlp, mng, hlb, hng, wout, lng, lnb]
    in_specs = [pl.BlockSpec((None, tb, D), lambda b, t: (b, t, 0))]
    in_specs += [_const_spec(c.shape) for c in consts]
    args = [x] + consts
    state_shapes = [(ML_H, ML_DK, ML_DV), (SUBLANES, ML_DK), (SUBLANES, LANES), (HG_H, HG_DK, HG_DV)]
    if has_state:
        c0, n0, m0, hs0 = state
        n0p = jnp.zeros((B, SUBLANES, ML_DK), F32).at[:, :ML_H, :].set(n0)
        m0p = jnp.zeros((B, SUBLANES, LANES), F32).at[:, :ML_H, :].set(
            jnp.broadcast_to(m0[:, :, None], (B, ML_H, LANES)))
        args += [c0, n0p, m0p, hs0]
        in_specs += [_batch_spec(s) for s in state_shapes]
    out_shape = [jax.ShapeDtypeStruct((B, T, D), x.dtype)]
    out_shape += [jax.ShapeDtypeStruct((B,) + s, x.dtype) for s in state_shapes]
    out_specs = [pl.BlockSpec((None, tb, D), lambda b, t: (b, t, 0))]
    out_specs += [_batch_spec(s) for s in state_shapes]
    scratch = [
        pltpu.VMEM((tb, ODD_COLS), F32),
        pltpu.VMEM((tb, 2 * D_MODEL), F32),
        pltpu.VMEM((ML_H, ML_DK, ML_DV), F32),
        pltpu.VMEM((SUBLANES, ML_DK), F32),
        pltpu.VMEM((SUBLANES, LANES), F32),
        pltpu.VMEM((HG_H // 2, HG_DV, 2 * HG_DK), F32),
        pltpu.VMEM((tb, LANES), F32),
        pltpu.VMEM(((tb // chunk) * ML_H, chunk, ML_DV), F32),
        pltpu.VMEM(((tb // chunk) * ML_H, ML_DK, ML_DV), F32),
        pltpu.VMEM(((tb // chunk) * ML_H, chunk + SUBLANES, LANES), F32),
    ]
    outs = pl.pallas_call(
        functools.partial(_odd_kernel, tb=tb, chunk=chunk, layer=layer, has_state=has_state),
        grid=(B, T // tb),
        in_specs=in_specs,
        out_specs=out_specs,
        out_shape=out_shape,
        scratch_shapes=scratch,
        compiler_params=pltpu.CompilerParams(
            dimension_semantics=("arbitrary", "arbitrary"), vmem_limit_bytes=VMEM_LIMIT_BYTES),
        name="odd_layer_state" if has_state else "odd_layer",
    )(*args)
    xo, co, no, mo, hso = outs
    return xo, co, no[:, :ML_H, :], mo[:, :ML_H, 0], hso


def _pad_lanes(cols):
    return jnp.pad(cols, ((0, 0), (0, LANES - cols.shape[1])))


def _even_params(j, w_in_even, rg_conv_w, rg_conv_b, rg_w_a, rg_b_a, rg_w_i, rg_b_i, rg_lambda,
                 gdn_conv_w, gdn_a_log, gdn_dt_bias, gdn_norm_g, w_out_even, ln_even_g, ln_even_b):
    w = w_in_even[j]
    xa, ga, qb, kb, vb = (w[:, i * 1024:(i + 1) * 1024] for i in range(5))
    small = w[:, 5120:5136]
    gb = w[:, 5136:6160]
    win = jnp.concatenate([xa, gb, ga, qb, kb, vb, _pad_lanes(small)], axis=1).astype(BF16)
    wai = jnp.concatenate([rg_w_a[j], rg_w_i[j]], axis=-1).astype(BF16)
    bai = jnp.stack([rg_b_a[j], rg_b_i[j]]).astype(F32)
    gdp = jnp.zeros((2, LANES), F32)
    gdp = gdp.at[0, GD_H:2 * GD_H].set(gdn_a_log[j]).at[1, GD_H:2 * GD_H].set(gdn_dt_bias[j])
    return (win, rg_conv_w[j], rg_conv_b[j][None, :], wai, bai, rg_lambda[j][None, :],
            gdn_conv_w[j], gdp, gdn_norm_g[j][None, :], w_out_even[j].astype(BF16),
            ln_even_g[j][None, :], ln_even_b[j][None, :])


def _odd_params(j, w_in_odd, ml_b_i, ml_b_f, ml_norm_g, hg_lb, hg_norm_g, w_out_odd, ln_odd_g, ln_odd_b):
    w = w_in_odd[j]
    mq, mk, mv = w[:, 0:512], w[:, 512:1024], w[:, 1024:2048]
    small = w[:, 2048:2056]
    mo, mg, hq, hf, hi, hgate = (w[:, 2056 + i * 1024:2056 + (i + 1) * 1024] for i in range(6))
    win = jnp.concatenate([mo, hgate, mg, mq, mk, mv, hq, hf, hi, _pad_lanes(small)], axis=1).astype(BF16)
    mlp = jnp.zeros((2, LANES), F32)
    mlp = mlp.at[0, 0:ML_H].set(ml_b_i[j]).at[1, ML_H:2 * ML_H].set(ml_b_f[j])
    return (win, mlp, ml_norm_g[j][None, :], hg_lb.astype(F32), hg_norm_g[j][None, :],
            w_out_odd[j].astype(BF16), ln_odd_g[j][None, :], ln_odd_b[j][None, :])


def _run(x, states, even_p, odd_p, tb_max):
    rg_h, rg_cv, gd_s, gd_cv, ml_c, ml_n, ml_m, hg_s = ([] for _ in range(8))
    for l in range(DEPTH):
        j = l // 2
        if l % 2 == 0:
            st = None if states is None else (states[0][j], states[1][j], states[2][j], states[3][j])
            x, h, cva, s, cvb = _even_layer(x, even_p[j], st, tb_max)
            rg_h.append(h); rg_cv.append(cva); gd_s.append(s); gd_cv.append(cvb)
        else:
            st = None if states is None else (states[4][j], states[5][j], states[6][j], states[7][j])
            x, c, n, m, s = _odd_layer(x, odd_p[j], j, st, tb_max)
            ml_c.append(c); ml_n.append(n); ml_m.append(m); hg_s.append(s)
    return (x, jnp.stack(rg_h), jnp.stack(rg_cv), jnp.stack(gd_s), jnp.stack(gd_cv),
            jnp.stack(ml_c), jnp.stack(ml_n), jnp.stack(ml_m), jnp.stack(hg_s))


def _forward(x_prompt, x_sample, states, even_w, odd_w, tb_max=TIME_BLOCK):
    n_even = even_w[0].shape[0]
    n_odd = odd_w[0].shape[0]
    even_p = [_even_params(j, *even_w) for j in range(n_even)]
    odd_p = [_odd_params(j, *odd_w) for j in range(n_odd)]
    p = _run(x_prompt, None, even_p, odd_p, tb_max)
    s = _run(x_sample, states, even_p, odd_p, tb_max)
    return (p[0], s[0]) + p[1:] + s[1:]


def kernel(x_prompt, x_sample, state_rglru_h, state_rglru_conv, state_gdn_S, state_gdn_conv, state_mlstm_C, state_mlstm_n, state_mlstm_m, state_hgrn_S, w_in_even, rg_conv_w, rg_conv_b, rg_w_a, rg_b_a, rg_w_i, rg_b_i, rg_lambda, gdn_conv_w, gdn_a_log, gdn_dt_bias, gdn_norm_g, w_out_even, ln_even_g, ln_even_b, w_in_odd, ml_b_i, ml_b_f, ml_norm_g, hg_lb, hg_norm_g, w_out_odd, ln_odd_g, ln_odd_b):
    states = (state_rglru_h, state_rglru_conv, state_gdn_S, state_gdn_conv,
              state_mlstm_C, state_mlstm_n, state_mlstm_m, state_hgrn_S)
    even_w = (w_in_even, rg_conv_w, rg_conv_b, rg_w_a, rg_b_a, rg_w_i, rg_b_i, rg_lambda,
              gdn_conv_w, gdn_a_log, gdn_dt_bias, gdn_norm_g, w_out_even, ln_even_g, ln_even_b)
    odd_w = (w_in_odd, ml_b_i, ml_b_f, ml_norm_g, hg_lb, hg_norm_g, w_out_odd, ln_odd_g, ln_odd_b)
    return _forward(x_prompt, x_sample, states, even_w, odd_w)
```

```python
import functools
import math

import jax
import jax.numpy as jnp
from jax import lax
from jax.experimental import pallas as pl
from jax.experimental.pallas import tpu as pltpu

F32 = jnp.float32
BF16 = jnp.bfloat16

D_MODEL = 1024
DEPTH = 4
CONV_W = 4
ALPHA = (2 * DEPTH) ** 0.25
EPS = 1e-5
RG_W = D_MODEL
RG_BLOCKS = 8
RG_BW = RG_W // RG_BLOCKS
RG_C = 8.0
GD_H = 8
GD_DK = D_MODEL // GD_H
GD_DV = D_MODEL // GD_H
GD_CONV = 3 * D_MODEL
ML_H = 4
ML_DK = D_MODEL // (2 * ML_H)
ML_DV = D_MODEL // ML_H
ML_QK = ML_H * ML_DK
HG_H = 8
HG_DK = D_MODEL // HG_H
HG_DV = D_MODEL // HG_H
HG_BLOCK = 16
CHUNK = 64
HG_UNROLL = 4
LOG2E = 1.4426950408889634

LANES = 128
SUBLANES = 8
CONV_PAD = SUBLANES
TIME_BLOCK = 256
VMEM_LIMIT_BYTES = 56 * 1024 * 1024

E_XA, E_GB, E_GA, E_Q, E_K, E_V, E_SM = 0, 1024, 2048, 3072, 4096, 5120, 6144
EVEN_COLS = E_SM + LANES
O_MO, O_HGATE, O_MG, O_MQ, O_MK, O_MV, O_HQ, O_HF, O_HI, O_SM = (
    0, 1024, 2048, 3072, 3584, 4096, 5120, 6144, 7168, 8192)
ODD_COLS = O_SM + LANES


def _dot(a, b):
    return jnp.dot(a.astype(BF16), b.astype(BF16), preferred_element_type=F32)


def _dot_nt(a, b):
    return lax.dot_general(a.astype(BF16), b.astype(BF16), (((1,), (1,)), ((), ())),
                           preferred_element_type=F32)


def _dot_tn(a, b):
    return lax.dot_general(a.astype(BF16), b.astype(BF16), (((0,), (0,)), ((), ())),
                           preferred_element_type=F32)


def _split2(x):
    hi = x.astype(BF16)
    lo = (x - hi.astype(F32)).astype(BF16)
    return hi, lo


def _dot_x3(a, b):
    a_hi, a_lo = _split2(a)
    b_hi, b_lo = _split2(b)
    acc = jnp.dot(a_hi, b_hi, preferred_element_type=F32)
    acc += jnp.dot(a_hi, b_lo, preferred_element_type=F32)
    acc += jnp.dot(a_lo, b_hi, preferred_element_type=F32)
    return acc


def _rows_from_cols(sel, m):
    hi = m.astype(BF16)
    r1 = m - hi.astype(F32)
    mid = r1.astype(BF16)
    lo = (r1 - mid.astype(F32)).astype(BF16)
    dn = (((1,), (1,)), ((), ()))
    out = lax.dot_general(sel, hi, dn, preferred_element_type=F32)
    out += lax.dot_general(sel, mid, dn, preferred_element_type=F32)
    out += lax.dot_general(sel, lo, dn, preferred_element_type=F32)
    return out


def _selector(lane0):
    r = lax.broadcasted_iota(jnp.int32, (SUBLANES, LANES), 0)
    c = lax.broadcasted_iota(jnp.int32, (SUBLANES, LANES), 1)
    return (c == r + lane0).astype(BF16)


def _sigmoid(x):
    return jax.nn.sigmoid(x)


def _silu(x):
    return x * jax.nn.sigmoid(x)


def _log1p_exp_neg_abs(x):
    return jnp.log(1.0 + jnp.exp(-jnp.abs(x)))


def _softplus(x):
    return jnp.maximum(x, 0.0) + _log1p_exp_neg_abs(x)


def _log_sigmoid(x):
    return jnp.minimum(x, 0.0) - _log1p_exp_neg_abs(x)


def _chunk_cumsum(x, chunk):
    n = x.shape[0]
    pos = lax.broadcasted_iota(jnp.int32, (n, 1), 0) % chunk
    d = 1
    while d < chunk:
        x = x + jnp.where(pos >= d, pltpu.roll(x, d, 0), 0.0)
        d *= 2
    return x


def _pair_tile_cumsum(x):
    n, w = x.shape
    x3 = x.reshape(n // SUBLANES, SUBLANES, w)
    sub = lax.broadcasted_iota(jnp.int32, (1, SUBLANES, 1), 1)
    d = 1
    while d < SUBLANES:
        x3 = x3 + jnp.where(sub >= d, pltpu.roll(x3, d, 1), 0.0)
        d *= 2
    x4 = x3.reshape(n // (2 * SUBLANES), 2, SUBLANES, w)
    first = x4[:, 0]
    second = x4[:, 1] + first[:, SUBLANES - 1:SUBLANES, :]
    return jnp.concatenate([first[:, None], second[:, None]], axis=1).reshape(n, w)


def _causal_conv(ext_ref, u, w_ref, tb):
    ext_ref[CONV_PAD:CONV_PAD + tb, :] = u
    base = CONV_PAD - (CONV_W - 1)
    nt8 = tb // SUBLANES
    wdt = u.shape[1]
    ext3 = ext_ref[...].reshape(nt8 + 1, SUBLANES, wdt)
    sub = lax.broadcasted_iota(jnp.int32, (1, SUBLANES, 1), 1)
    y = u.reshape(nt8, SUBLANES, wdt) * w_ref[CONV_W - 1:CONV_W, :]
    for s in range(1, CONV_W):
        r3 = pltpu.roll(ext3, s, 1)
        y = y + jnp.where(sub < s, r3[:nt8], r3[1:]) * w_ref[CONV_W - 1 - s:CONV_W - s, :]
    ext_ref[base:CONV_PAD, :] = ext_ref[base + tb:CONV_PAD + tb, :]
    return y.reshape(tb, wdt)


def _layer_norm_rows(x, g, b):
    mu = jnp.mean(x, axis=-1, keepdims=True)
    xc = x - mu
    var = jnp.mean(xc * xc, axis=-1, keepdims=True)
    return xc * lax.rsqrt(var + EPS) * g + b


def _tri_masks(c):
    r = lax.broadcasted_iota(jnp.int32, (c, c), 0)
    s = lax.broadcasted_iota(jnp.int32, (c, c), 1)
    return s <= r, s < r, (s == r).astype(F32)


def _unit_lower_inverse(a, eye):
    c = a.shape[0]
    p = eye - a
    q = _dot_x3(a, a)
    span = 2
    while span < c:
        p = p + _dot_x3(p, q)
        span *= 2
        if span < c:
            q = _dot_x3(q, q)
    return p


def _even_kernel(*refs, tb, chunk, has_state):
    (x_ref, win_ref, cwa_ref, cba_ref, wai_ref, bai_ref, lam_ref, cwb_ref, gdp_ref, gng_ref,
     wout_ref, lng_ref, lnb_ref) = refs[:13]
    pos = 13
    if has_state:
        h0_ref, cva0_ref, s0_ref, cvb0_ref = refs[pos:pos + 4]
        pos += 4
    xo_ref, ho_ref, cvao_ref, so_ref, cvbo_ref = refs[pos:pos + 5]
    pos += 5
    (z_ref, y_ref, exta_ref, extb_ref, h_ref, s_ref, g_ref, beta_ref,
     p_ref, q_ref, qkm_ref) = refs[pos:pos + 11]

    t = pl.program_id(1)
    nt = pl.num_programs(1)
    base = CONV_PAD - (CONV_W - 1)

    @pl.when(t == 0)
    def _init():
        exta_ref[0:CONV_PAD, :] = jnp.zeros((CONV_PAD, RG_W), F32)
        extb_ref[0:CONV_PAD, :] = jnp.zeros((CONV_PAD, GD_CONV), F32)
        if has_state:
            h_ref[...] = h0_ref[...]
            exta_ref[base:CONV_PAD, :] = cva0_ref[...]
            extb_ref[base:CONV_PAD, :] = cvb0_ref[...]
            s_ref[...] = s0_ref[...]
        else:
            h_ref[...] = jnp.zeros_like(h_ref)
            s_ref[...] = jnp.zeros_like(s_ref)

    x = x_ref[...]
    xb = x.astype(BF16)
    ncol = 512
    for n0 in range(0, EVEN_COLS, ncol):
        n1 = min(n0 + ncol, EVEN_COLS)
        z_ref[:, n0:n1] = jnp.dot(xb, win_ref[:, n0:n1], preferred_element_type=F32)

    xa = _causal_conv(exta_ref, z_ref[:, E_XA:E_XA + RG_W], cwa_ref, tb) + cba_ref[...]
    gates = []
    for g in range(RG_BLOCKS):
        gates.append(_dot(xa[:, g * RG_BW:(g + 1) * RG_BW], wai_ref[g]))
    pre_r = jnp.concatenate([gt[:, :RG_BW] for gt in gates], axis=-1) + bai_ref[0:1, :]
    pre_i = jnp.concatenate([gt[:, RG_BW:] for gt in gates], axis=-1) + bai_ref[1:2, :]
    r = _sigmoid(pre_r)
    i = _sigmoid(pre_i)
    log_a = (-RG_C) * r * _softplus(-lam_ref[...])
    a = jnp.exp(log_a)
    th = jnp.tanh(log_a)
    u = jnp.sqrt((-2.0) * th / (1.0 - th)) * (i * xa)
    nt8 = tb // SUBLANES
    a3 = a.reshape(nt8, SUBLANES, RG_W)
    u3 = u.reshape(nt8, SUBLANES, RG_W)
    sub = lax.broadcasted_iota(jnp.int32, (1, SUBLANES, 1), 1)
    d = 1
    while d < SUBLANES:
        m = sub >= d
        u3 = a3 * jnp.where(m, pltpu.roll(u3, d, 1), 0.0) + u3
        a3 = a3 * jnp.where(m, pltpu.roll(a3, d, 1), 1.0)
        d *= 2
    hprev = h_ref[...]
    for r in range(nt8):
        ht = u3[r] + a3[r] * hprev
        hprev = ht[SUBLANES - 1:SUBLANES, :]
        rows8 = slice(r * SUBLANES, (r + 1) * SUBLANES)
        y_ref[rows8, 0:RG_W] = ht * _silu(z_ref[rows8, E_GA:E_GA + RG_W])
    h_ref[...] = hprev

    qkv = _silu(_causal_conv(extb_ref, z_ref[:, E_Q:E_Q + GD_CONV], cwb_ref, tb))
    z_ref[:, E_Q:E_Q + GD_CONV] = qkv
    sm = z_ref[:, E_SM:E_SM + LANES]
    beta_ref[...] = _sigmoid(sm)
    g_all = -jnp.exp(gdp_ref[0:1, :]) * _softplus(sm + gdp_ref[1:2, :])
    g_ref[...] = _chunk_cumsum(g_all, chunk)
    for h in range(GD_H):
        lq = slice(E_Q + h * GD_DK, E_Q + (h + 1) * GD_DK)
        lk = slice(E_K + h * GD_DK, E_K + (h + 1) * GD_DK)
        qh = z_ref[:, lq]
        z_ref[:, lq] = qh * (lax.rsqrt(jnp.sum(qh * qh, axis=-1, keepdims=True) + 1e-6) * (GD_DK ** -0.5))
        kh = z_ref[:, lk]
        z_ref[:, lk] = kh * lax.rsqrt(jnp.sum(kh * kh, axis=-1, keepdims=True) + 1e-6)

    gw = 2 * LANES
    hg = gw // chunk
    ng = GD_H // hg
    nc = tb // chunk
    lc = chunk.bit_length() - 1
    assert 1 << lc == chunk and hg * chunk == gw and ng * hg == GD_H

    def _iota(shape, dim):
        return lax.broadcasted_iota(jnp.int32, shape, dim)

    ri = _iota((chunk, gw), 0)
    cj = jnp.bitwise_and(_iota((chunk, gw), 1), chunk - 1)
    tril_cat = cj <= ri
    strict_cat = cj < ri
    eye_cat = (cj == ri).astype(F32)
    bdmask = (lax.shift_right_logical(_iota((gw, gw), 0), lc)
              == lax.shift_right_logical(_iota((gw, gw), 1), lc))
    kw = hg * GD_DK
    bdmask_k = (lax.shift_right_logical(_iota((gw, kw), 0), lc)
                == lax.shift_right_logical(_iota((gw, kw), 1), 7))
    ex = (_iota((LANES, 2 * GD_H * chunk), 0)
          == lax.shift_right_logical(_iota((LANES, 2 * GD_H * chunk), 1), lc)).astype(BF16)
    lane = _iota((1, LANES), 1)
    sel = _selector(GD_H)

    def block_diag(xcat):
        return jnp.where(bdmask, jnp.concatenate([xcat] * hg, axis=0), 0.0).astype(BF16)

    for c in range(nc):
        rows = slice(c * chunk, (c + 1) * chunk)
        gblk = g_ref[rows, :]
        bblk = beta_ref[rows, :]
        grow = _rows_from_cols(sel, gblk)
        gr_all = jnp.concatenate([grow[h:h + 1, :] for h in range(GD_H)], axis=-1)
        m = jnp.where(lane < GD_H, bblk, gblk)
        hi = m.astype(BF16)
        r1 = m - hi.astype(F32)
        mid = r1.astype(BF16)
        lo = (r1 - mid.astype(F32)).astype(BF16)
        ex3 = (jnp.dot(hi, ex, preferred_element_type=F32) + jnp.dot(mid, ex, preferred_element_type=F32)
               + jnp.dot(lo, ex, preferred_element_type=F32))
        for g in range(ng):
            item = c * ng + g
            bcat = ex3[:, g * gw:(g + 1) * gw]
            gcat = ex3[:, GD_H * chunk + g * gw:GD_H * chunk + (g + 1) * gw]
            dec = jnp.exp(jnp.where(tril_cat, gcat - gr_all[:, g * gw:(g + 1) * gw], -jnp.inf))
            k_g = z_ref[rows, E_K + g * kw:E_K + (g + 1) * kw]
            q_g = z_ref[rows, E_Q + g * kw:E_Q + (g + 1) * kw]
            kq = jnp.concatenate([k_g, q_g], axis=0)
            kbd = jnp.where(bdmask_k, jnp.concatenate([k_g] * hg, axis=0), 0.0)
            res = _dot_nt(kq, kbd)
            amat = jnp.where(strict_cat, bcat * res[:chunk] * dec, 0.0)
            qkm_ref[item] = res[chunk:] * dec
            p_ref[item] = eye_cat - amat
            q_ref[item] = amat

    for item in range(nc * ng):
        amat = q_ref[item]
        q_ref[item] = _dot(amat, block_diag(amat))
    span = 2
    while span < chunk:
        last = 2 * span >= chunk
        for item in range(nc * ng):
            p = p_ref[item]
            q = q_ref[item]
            qbd = block_diag(q)
            if last:
                p_ref[item] = p + _dot(p, qbd)
            else:
                pq = _dot(jnp.concatenate([p, q], axis=0), qbd)
                p_ref[item] = p + pq[:chunk]
                q_ref[item] = pq[chunk:]
        span *= 2

    for c in range(nc):
        rows = slice(c * chunk, (c + 1) * chunk)
        gblk = g_ref[rows, :]
        bblk = beta_ref[rows, :]
        first = []
        for h in range(GD_H):
            lq = slice(E_Q + h * GD_DK, E_Q + (h + 1) * GD_DK)
            lk = slice(E_K + h * GD_DK, E_K + (h + 1) * GD_DK)
            gc = gblk[:, GD_H + h:GD_H + h + 1]
            bc = bblk[:, h:h + 1]
            eg = jnp.exp(gc)
            kh = z_ref[rows, lk]
            s_old = s_ref[h]
            r1 = _dot(jnp.concatenate([kh * (bc * eg), z_ref[rows, lq] * eg], axis=0), s_old)
            first.append((gc, bc, kh, s_old, r1))
        second = []
        for h in range(GD_H):
            gc, bc, kh, s_old, r1 = first[h]
            item = c * ng + h // hg
            li = (h % hg) * chunk
            lv = slice(E_V + h * GD_DV, E_V + (h + 1) * GD_DV)
            dlt = _dot(p_ref[item, :, li:li + chunk], z_ref[rows, lv] * bc - r1[:chunk])
            second.append(dlt)
        for h in range(GD_H):
            gc, bc, kh, s_old, r1 = first[h]
            dlt = second[h]
            item = c * ng + h // hg
            li = (h % hg) * chunk
            o = r1[chunk:] + _dot(qkm_ref[item, :, li:li + chunk], dlt)
            gl = gc[chunk - 1:chunk, :]
            s_ref[h] = jnp.exp(gl) * s_old + _dot_tn(kh * jnp.exp(gl - gc), dlt)
            on = o * lax.rsqrt(jnp.mean(o * o, axis=-1, keepdims=True) + EPS) * gng_ref[...]
            lg = slice(E_GB + h * GD_DV, E_GB + (h + 1) * GD_DV)
            ly = slice(RG_W + h * GD_DV, RG_W + (h + 1) * GD_DV)
            y_ref[rows, ly] = on * _silu(z_ref[rows, lg])

    y = jnp.dot(y_ref[...].astype(BF16), wout_ref[...], preferred_element_type=F32)
    xo_ref[...] = _layer_norm_rows(ALPHA * x + y, lng_ref[...], lnb_ref[...])

    @pl.when(t == nt - 1)
    def _fin():
        ho_ref[...] = h_ref[...]
        cvao_ref[...] = exta_ref[base:CONV_PAD, :]
        cvbo_ref[...] = extb_ref[base:CONV_PAD, :]
        so_ref[...] = s_ref[...]


def _odd_kernel(*refs, tb, chunk, layer, has_state):
    (x_ref, win_ref, mlp_ref, mng_ref, hlb_ref, hng_ref, wout_ref, lng_ref, lnb_ref) = refs[:9]
    pos = 9
    if has_state:
        c0_ref, n0_ref, m0_ref, hs0_ref = refs[pos:pos + 4]
        pos += 4
    xo_ref, co_ref, no_ref, mo_ref, hso_ref = refs[pos:pos + 5]
    pos += 5
    z_ref, y_ref, c_ref, n_ref, m_ref, st_ref, gi_ref, a1_ref, kv_ref, loc_ref, oi_ref = refs[pos:pos + 11]

    t = pl.program_id(1)
    nt = pl.num_programs(1)
    hp = HG_H // 2

    @pl.when(t == 0)
    def _init():
        if has_state:
            c_ref[...] = c0_ref[...]
            n_ref[...] = n0_ref[...]
            m_ref[...] = m0_ref[...]
            for h in range(HG_H):
                st_ref[h // 2, :, (h % 2) * HG_DK:(h % 2 + 1) * HG_DK] = hs0_ref[h].T
        else:
            c_ref[...] = jnp.zeros_like(c_ref)
            n_ref[...] = jnp.zeros_like(n_ref)
            m_ref[...] = jnp.zeros_like(m_ref)
            st_ref[...] = jnp.zeros_like(st_ref)

    x = x_ref[...]
    xb = x.astype(BF16)
    ncol = 512
    for n0 in range(0, ODD_COLS, ncol):
        n1 = min(n0 + ncol, ODD_COLS)
        z_ref[:, n0:n1] = jnp.dot(xb, win_ref[:, n0:n1], preferred_element_type=F32)

    sm = z_ref[:, O_SM:O_SM + LANES]
    ig_all = sm + mlp_ref[0:1, :]
    lf_all = _log_sigmoid(sm + mlp_ref[1:2, :])
    b_all = _chunk_cumsum(lf_all, chunk)
    lane = lax.broadcasted_iota(jnp.int32, (1, LANES), 1)
    gi_ref[...] = jnp.where(lane < ML_H, ig_all, b_all)
    sel = _selector(0)
    tril, _, _ = _tri_masks(chunk)
    nc = tb // chunk

    for c in range(nc):
        rows = slice(c * chunk, (c + 1) * chunk)
        gblk = gi_ref[rows, :]
        grow = _rows_from_cols(sel, gblk)
        for h in range(ML_H):
            item = c * ML_H + h
            qh = z_ref[rows, O_MQ + h * ML_DK:O_MQ + (h + 1) * ML_DK] * (ML_DK ** -0.5)
            kh = z_ref[rows, O_MK + h * ML_DK:O_MK + (h + 1) * ML_DK]
            vh = z_ref[rows, O_MV + h * ML_DV:O_MV + (h + 1) * ML_DV].astype(BF16)
            igc = gblk[:, h:h + 1]
            bc = gblk[:, ML_H + h:ML_H + h + 1]
            dm = jnp.where(tril, bc - grow[ML_H + h:ML_H + h + 1, :] + grow[h:h + 1, :], -jnp.inf)
            md = jnp.max(dm, axis=-1, keepdims=True)
            qkl = _dot_nt(qh, kh) * jnp.exp(dm - md)
            a1_ref[item] = _dot(qkl, vh)
            kwl = kh * jnp.exp(bc[chunk - 1:chunk, :] - bc + igc - md[chunk - 1:chunk, :])
            kv_ref[item] = _dot_tn(kwl, vh)
            loc_ref[item, 0:chunk, 0:1] = md
            loc_ref[item, 0:chunk, 1:2] = jnp.sum(qkl, axis=-1, keepdims=True)
            loc_ref[item, chunk:chunk + 1, :] = jnp.sum(kwl, axis=0, keepdims=True)

    for c in range(nc):
        rows = slice(c * chunk, (c + 1) * chunk)
        gblk = gi_ref[rows, :]
        for h in range(ML_H):
            item = c * ML_H + h
            qh = z_ref[rows, O_MQ + h * ML_DK:O_MQ + (h + 1) * ML_DK] * (ML_DK ** -0.5)
            bc = gblk[:, ML_H + h:ML_H + h + 1]
            md = loc_ref[item, 0:chunk, 0:1]
            rs = loc_ref[item, 0:chunk, 1:2]
            kn = loc_ref[item, chunk:chunk + 1, :]
            m_old = m_ref[h:h + 1, 0:1]
            c_old = c_ref[h]
            n_old = n_ref[h:h + 1, :]
            inter = bc + m_old
            mt = jnp.maximum(inter, md)
            sc = jnp.exp(inter - mt)
            sd = jnp.exp(md - mt)
            num = sc * _dot(qh, c_old) + sd * a1_ref[item]
            den = sc * jnp.sum(qh * n_old, axis=-1, keepdims=True) + sd * rs
            hh = num / jnp.maximum(jnp.abs(den), jnp.exp(-mt))
            ml = mt[chunk - 1:chunk, :]
            sl = jnp.exp(inter[chunk - 1:chunk, :] - ml)
            sdl = sd[chunk - 1:chunk, :]
            c_ref[h] = sl * c_old + sdl * kv_ref[item]
            n_ref[h:h + 1, :] = sl * n_old + sdl * kn
            m_ref[h:h + 1, :] = jnp.broadcast_to(ml, (1, LANES))
            mu = jnp.mean(hh, axis=-1, keepdims=True)
            hc = hh - mu
            var = jnp.mean(hc * hc, axis=-1, keepdims=True)
            hn = hc * lax.rsqrt(var + EPS) * mng_ref[...]
            lo = slice(O_MO + h * ML_DV, O_MO + (h + 1) * ML_DV)
            lg = slice(O_MG + h * ML_DV, O_MG + (h + 1) * ML_DV)
            y_ref[rows, h * ML_DV:(h + 1) * ML_DV] = _sigmoid(z_ref[rows, lo]) * hn * _silu(z_ref[rows, lg])

    lbp = hlb_ref[...]
    e = jnp.exp(lbp - jnp.max(lbp, axis=0, keepdims=True))
    smx = e / jnp.sum(e, axis=0, keepdims=True)
    lb = smx[0:1, :] * 0.0
    for jj in range(1, layer + 1):
        lb = lb + smx[jj:jj + 1, :]
    hf = z_ref[:, O_HF:O_HF + D_MODEL]
    la = jnp.log(lb)
    lbv = jnp.log1p(-lb) + _log_sigmoid(hf)
    logf = jnp.maximum(la, lbv) + _log1p_exp_neg_abs(la - lbv)
    assert HG_BLOCK == 2 * SUBLANES
    g2 = _pair_tile_cumsum(logf) * LOG2E
    z_ref[:, O_MG:O_MG + D_MODEL] = g2
    z_ref[:, O_HF:O_HF + D_MODEL] = g2 - jnp.log((1.0 - lb) * _sigmoid(-hf)) * LOG2E
    z_ref[:, O_HQ:O_HQ + D_MODEL] = _silu(z_ref[:, O_HQ:O_HQ + D_MODEL]) * (HG_DK ** -0.5)
    half = HG_BLOCK // 2
    rowi = lax.broadcasted_iota(jnp.int32, (HG_BLOCK, 1), 0)
    kcat = HG_BLOCK * HG_DK
    sel_j = (lax.broadcasted_iota(jnp.int32, (HG_BLOCK, kcat), 0)
             == lax.shift_right_logical(lax.broadcasted_iota(jnp.int32, (HG_BLOCK, kcat), 1), 7)).astype(BF16)
    pair_mask = (lax.shift_right_logical(lax.broadcasted_iota(jnp.int32, (2 * HG_BLOCK, 2 * HG_DK), 0), 4)
                 == lax.shift_right_logical(lax.broadcasted_iota(jnp.int32, (2 * HG_BLOCK, 2 * HG_DK), 1), 7))
    zero_half = jnp.zeros((half, HG_DK), F32)
    zero_rows = jnp.zeros((LANES - HG_BLOCK, HG_H * HG_BLOCK), F32)

    ub = min(HG_UNROLL, tb // HG_BLOCK)

    def intra_group(r0):
        pcats = []
        for u in range(ub):
            rows = pl.ds(r0 + u * HG_BLOCK, HG_BLOCK)
            gb = z_ref[rows, O_MG:O_MG + D_MODEL]
            hb = z_ref[rows, O_HF:O_HF + D_MODEL]
            qb = z_ref[rows, O_HQ:O_HQ + D_MODEL]
            pieces = []
            for j in range(HG_BLOCK):
                lo = 0 if j < half else half
                pj = qb[lo:] * jnp.exp2(jnp.where(rowi[lo:] >= j, gb[lo:] - hb[j:j + 1, :], -jnp.inf))
                tiles = []
                for h in range(HG_H):
                    if lo:
                        tiles.append(zero_half)
                    tiles.append(pj[:, h * HG_DK:(h + 1) * HG_DK])
                pieces.append(jnp.concatenate(tiles, axis=0))
            pcats.append(jnp.concatenate(pieces, axis=1).astype(BF16))
        atts = []
        for u in range(ub):
            att_t = lax.dot_general(sel_j, pcats[u], (((1,), (1,)), ((), ())),
                                    preferred_element_type=F32)
            atts.append(jnp.concatenate([att_t, zero_rows], axis=0).T)
        for u in range(ub):
            rows = pl.ds(r0 + u * HG_BLOCK, HG_BLOCK)
            vb = z_ref[rows, O_HI:O_HI + D_MODEL]
            for h in range(HG_H):
                att_h = atts[u][h * HG_BLOCK:(h + 1) * HG_BLOCK, 0:HG_BLOCK]
                oi_ref[rows, h * HG_DV:(h + 1) * HG_DV] = _dot(att_h, vb[:, h * HG_DV:(h + 1) * HG_DV])

    def state_group(r0):
        for u in range(ub):
            rows = pl.ds(r0 + u * HG_BLOCK, HG_BLOCK)
            gb = z_ref[rows, O_MG:O_MG + D_MODEL]
            gl = gb[HG_BLOCK - 1:HG_BLOCK, :]
            egl = jnp.exp2(gl)
            qe = z_ref[rows, O_HQ:O_HQ + D_MODEL] * jnp.exp2(gb)
            ke = jnp.exp2(gl - z_ref[rows, O_HF:O_HF + D_MODEL])
            vb = z_ref[rows, O_HI:O_HI + D_MODEL]
            for h in range(HG_H):
                ly = slice(D_MODEL + h * HG_DV, D_MODEL + (h + 1) * HG_DV)
                st_old = st_ref[h // 2, :, (h % 2) * HG_DK:(h % 2 + 1) * HG_DK]
                o = oi_ref[rows, h * HG_DV:(h + 1) * HG_DV] + _dot_nt(qe[:, h * HG_DK:(h + 1) * HG_DK], st_old)
                on = o * lax.rsqrt(jnp.mean(o * o, axis=-1, keepdims=True) + EPS) * hng_ref[...]
                lgt = slice(O_HGATE + h * HG_DV, O_HGATE + (h + 1) * HG_DV)
                y_ref[rows, ly] = on * _silu(z_ref[rows, lgt])
            for p in range(hp):
                lp = slice(2 * p * HG_DK, (2 * p + 2) * HG_DK)
                vst = jnp.concatenate([vb[:, 2 * p * HG_DV:(2 * p + 1) * HG_DV],
                                       vb[:, (2 * p + 1) * HG_DV:(2 * p + 2) * HG_DV]], axis=0)
                kep = ke[:, lp]
                rhs = jnp.where(pair_mask, jnp.concatenate([kep, kep], axis=0), 0.0)
                st_ref[p] = egl[:, lp] * st_ref[p] + _dot_tn(vst, rhs)

    grp = ub * HG_BLOCK
    ngrp = tb // grp
    intra_group(0)

    def pipe_body(gi, carry):
        r0 = pl.multiple_of(gi * grp, grp)
        state_group(r0)
        intra_group(r0 + grp)
        return carry

    lax.fori_loop(0, ngrp - 1, pipe_body, 0)
    state_group((ngrp - 1) * grp)

    y = jnp.dot(y_ref[...].astype(BF16), wout_ref[...], preferred_element_type=F32)
    xo_ref[...] = _layer_norm_rows(ALPHA * x + y, lng_ref[...], lnb_ref[...])

    @pl.when(t == nt - 1)
    def _fin():
        co_ref[...] = c_ref[...]
        no_ref[...] = n_ref[...]
        mo_ref[...] = m_ref[...]
        for h in range(HG_H):
            hso_ref[h] = st_ref[h // 2, :, (h % 2) * HG_DK:(h % 2 + 1) * HG_DK].T


def _const_spec(shape):
    nd = len(shape)
    return pl.BlockSpec(shape, lambda b, t: (0,) * nd, pipeline_mode=pl.Buffered(1))


def _batch_spec(shape):
    nd = len(shape)
    return pl.BlockSpec((None,) + tuple(shape), lambda b, t: (b,) + (0,) * nd)


def _time_block(T, tb_max):
    tb = min(T, tb_max)
    while T % tb:
        tb -= 1
    return tb


def _chunk_len(tb, max_blk):
    c = min(max_blk, tb)
    while tb % c:
        c -= 1
    return c


def _even_layer(x, params, state, tb_max):
    B, T, D = x.shape
    tb = _time_block(T, tb_max)
    chunk = _chunk_len(tb, CHUNK)
    has_state = state is not None
    (win, cwa, cba, wai, bai, lam, cwb, gdp, gng, wout, lng, lnb) = params
    consts = [win, cwa, cba, wai, bai, lam, cwb, gdp, gng, wout, lng, lnb]
    in_specs = [pl.BlockSpec((None, tb, D), lambda b, t: (b, t, 0))]
    in_specs += [_const_spec(c.shape) for c in consts]
    args = [x] + consts
    state_shapes = [(1, RG_W), (CONV_W - 1, RG_W), (GD_H, GD_DK, GD_DV), (CONV_W - 1, GD_CONV)]
    if has_state:
        h0, cva0, s0, cvb0 = state
        args += [h0.reshape(B, 1, RG_W), cva0, s0, cvb0]
        in_specs += [_batch_spec(s) for s in state_shapes]
    out_shape = [jax.ShapeDtypeStruct((B, T, D), x.dtype)]
    out_shape += [jax.ShapeDtypeStruct((B,) + s, x.dtype) for s in state_shapes]
    out_specs = [pl.BlockSpec((None, tb, D), lambda b, t: (b, t, 0))]
    out_specs += [_batch_spec(s) for s in state_shapes]
    scratch = [
        pltpu.VMEM((tb, EVEN_COLS), F32),
        pltpu.VMEM((tb, 2 * D_MODEL), F32),
        pltpu.VMEM((CONV_PAD + tb, RG_W), F32),
        pltpu.VMEM((CONV_PAD + tb, GD_CONV), F32),
        pltpu.VMEM((1, RG_W), F32),
        pltpu.VMEM((GD_H, GD_DK, GD_DV), F32),
        pltpu.VMEM((tb, LANES), F32),
        pltpu.VMEM((tb, LANES), F32),
    ]
    n_items = (tb // chunk) * (GD_H * chunk // (2 * LANES))
    scratch += [pltpu.VMEM((n_items, chunk, 2 * LANES), F32) for _ in range(3)]
    outs = pl.pallas_call(
        functools.partial(_even_kernel, tb=tb, chunk=chunk, has_state=has_state),
        grid=(B, T // tb),
        in_specs=in_specs,
        out_specs=out_specs,
        out_shape=out_shape,
        scratch_shapes=scratch,
        compiler_params=pltpu.CompilerParams(
            dimension_semantics=("arbitrary", "arbitrary"), vmem_limit_bytes=VMEM_LIMIT_BYTES),
        name="even_layer_state" if has_state else "even_layer",
    )(*args)
    xo, ho, cvao, so, cvbo = outs
    return xo, ho.reshape(B, RG_W), cvao, so, cvbo


def _odd_layer(x, params, layer, state, tb_max):
    B, T, D = x.shape
    tb = _time_block(T, tb_max)
    chunk = _chunk_len(tb, CHUNK)
    has_state = state is not None
    (win, mlp, mng, hlb, hng, wout, lng, lnb) = params
    consts = [win, mlp, mng, hlb, hng, wout, lng, lnb]
    in_specs = [pl.BlockSpec((None, tb, D), lambda b, t: (b, t, 0))]
    in_specs += [_const_spec(c.shape) for c in consts]
    args = [x] + consts
    state_shapes = [(ML_H, ML_DK, ML_DV), (SUBLANES, ML_DK), (SUBLANES, LANES), (HG_H, HG_DK, HG_DV)]
    if has_state:
        c0, n0, m0, hs0 = state
        n0p = jnp.zeros((B, SUBLANES, ML_DK), F32).at[:, :ML_H, :].set(n0)
        m0p = jnp.zeros((B, SUBLANES, LANES), F32).at[:, :ML_H, :].set(
            jnp.broadcast_to(m0[:, :, None], (B, ML_H, LANES)))
        args += [c0, n0p, m0p, hs0]
        in_specs += [_batch_spec(s) for s in state_shapes]
    out_shape = [jax.ShapeDtypeStruct((B, T, D), x.dtype)]
    out_shape += [jax.ShapeDtypeStruct((B,) + s, x.dtype) for s in state_shapes]
    out_specs = [pl.BlockSpec((None, tb, D), lambda b, t: (b, t, 0))]
    out_specs += [_batch_spec(s) for s in state_shapes]
    scratch = [
        pltpu.VMEM((tb, ODD_COLS), F32),
        pltpu.VMEM((tb, 2 * D_MODEL), F32),
        pltpu.VMEM((ML_H, ML_DK, ML_DV), F32),
        pltpu.VMEM((SUBLANES, ML_DK), F32),
        pltpu.VMEM((SUBLANES, LANES), F32),
        pltpu.VMEM((HG_H // 2, HG_DV, 2 * HG_DK), F32),
        pltpu.VMEM((tb, LANES), F32),
        pltpu.VMEM(((tb // chunk) * ML_H, chunk, ML_DV), F32),
        pltpu.VMEM(((tb // chunk) * ML_H, ML_DK, ML_DV), F32),
        pltpu.VMEM(((tb // chunk) * ML_H, chunk + SUBLANES, LANES), F32),
        pltpu.VMEM((tb, D_MODEL), F32),
    ]
    outs = pl.pallas_call(
        functools.partial(_odd_kernel, tb=tb, chunk=chunk, layer=layer, has_state=has_state),
        grid=(B, T // tb),
        in_specs=in_specs,
        out_specs=out_specs,
        out_shape=out_shape,
        scratch_shapes=scratch,
        compiler_params=pltpu.CompilerParams(
            dimension_semantics=("arbitrary", "arbitrary"), vmem_limit_bytes=VMEM_LIMIT_BYTES),
        name="odd_layer_state" if has_state else "odd_layer",
    )(*args)
    xo, co, no, mo, hso = outs
    return xo, co, no[:, :ML_H, :], mo[:, :ML_H, 0], hso


def _pad_lanes(cols):
    return jnp.pad(cols, ((0, 0), (0, LANES - cols.shape[1])))


def _even_params(j, w_in_even, rg_conv_w, rg_conv_b, rg_w_a, rg_b_a, rg_w_i, rg_b_i, rg_lambda,
                 gdn_conv_w, gdn_a_log, gdn_dt_bias, gdn_norm_g, w_out_even, ln_even_g, ln_even_b):
    w = w_in_even[j]
    xa, ga, qb, kb, vb = (w[:, i * 1024:(i + 1) * 1024] for i in range(5))
    small = w[:, 5120:5136]
    gb = w[:, 5136:6160]
    win = jnp.concatenate([xa, gb, ga, qb, kb, vb, _pad_lanes(small)], axis=1).astype(BF16)
    wai = jnp.concatenate([rg_w_a[j], rg_w_i[j]], axis=-1).astype(BF16)
    bai = jnp.stack([rg_b_a[j], rg_b_i[j]]).astype(F32)
    gdp = jnp.zeros((2, LANES), F32)
    gdp = gdp.at[0, GD_H:2 * GD_H].set(gdn_a_log[j]).at[1, GD_H:2 * GD_H].set(gdn_dt_bias[j])
    return (win, rg_conv_w[j], rg_conv_b[j][None, :], wai, bai, rg_lambda[j][None, :],
            gdn_conv_w[j], gdp, gdn_norm_g[j][None, :], w_out_even[j].astype(BF16),
            ln_even_g[j][None, :], ln_even_b[j][None, :])


def _odd_params(j, w_in_odd, ml_b_i, ml_b_f, ml_norm_g, hg_lb, hg_norm_g, w_out_odd, ln_odd_g, ln_odd_b):
    w = w_in_odd[j]
    mq, mk, mv = w[:, 0:512], w[:, 512:1024], w[:, 1024:2048]
    small = w[:, 2048:2056]
    mo, mg, hq, hf, hi, hgate = (w[:, 2056 + i * 1024:2056 + (i + 1) * 1024] for i in range(6))
    win = jnp.concatenate([mo, hgate, mg, mq, mk, mv, hq, hf, hi, _pad_lanes(small)], axis=1).astype(BF16)
    mlp = jnp.zeros((2, LANES), F32)
    mlp = mlp.at[0, 0:ML_H].set(ml_b_i[j]).at[1, ML_H:2 * ML_H].set(ml_b_f[j])
    return (win, mlp, ml_norm_g[j][None, :], hg_lb.astype(F32), hg_norm_g[j][None, :],
            w_out_odd[j].astype(BF16), ln_odd_g[j][None, :], ln_odd_b[j][None, :])


def _run(x, states, even_p, odd_p, tb_max):
    rg_h, rg_cv, gd_s, gd_cv, ml_c, ml_n, ml_m, hg_s = ([] for _ in range(8))
    for l in range(DEPTH):
        j = l // 2
        if l % 2 == 0:
            st = None if states is None else (states[0][j], states[1][j], states[2][j], states[3][j])
            x, h, cva, s, cvb = _even_layer(x, even_p[j], st, tb_max)
            rg_h.append(h); rg_cv.append(cva); gd_s.append(s); gd_cv.append(cvb)
        else:
            st = None if states is None else (states[4][j], states[5][j], states[6][j], states[7][j])
            x, c, n, m, s = _odd_layer(x, odd_p[j], j, st, tb_max)
            ml_c.append(c); ml_n.append(n); ml_m.append(m); hg_s.append(s)
    return (x, jnp.stack(rg_h), jnp.stack(rg_cv), jnp.stack(gd_s), jnp.stack(gd_cv),
            jnp.stack(ml_c), jnp.stack(ml_n), jnp.stack(ml_m), jnp.stack(hg_s))


def _forward(x_prompt, x_sample, states, even_w, odd_w, tb_max=TIME_BLOCK):
    n_even = even_w[0].shape[0]
    n_odd = odd_w[0].shape[0]
    even_p = [_even_params(j, *even_w) for j in range(n_even)]
    odd_p = [_odd_params(j, *odd_w) for j in range(n_odd)]
    p = _run(x_prompt, None, even_p, odd_p, tb_max)
    s = _run(x_sample, states, even_p, odd_p, tb_max)
    return (p[0], s[0]) + p[1:] + s[1:]


def kernel(x_prompt, x_sample, state_rglru_h, state_rglru_conv, state_gdn_S, state_gdn_conv, state_mlstm_C, state_mlstm_n, state_mlstm_m, state_hgrn_S, w_in_even, rg_conv_w, rg_conv_b, rg_w_a, rg_b_a, rg_w_i, rg_b_i, rg_lambda, gdn_conv_w, gdn_a_log, gdn_dt_bias, gdn_norm_g, w_out_even, ln_even_g, ln_even_b, w_in_odd, ml_b_i, ml_b_f, ml_norm_g, hg_lb, hg_norm_g, w_out_odd, ln_odd_g, ln_odd_b):
    states = (state_rglru_h, state_rglru_conv, state_gdn_S, state_gdn_conv,
              state_mlstm_C, state_mlstm_n, state_mlstm_m, state_hgrn_S)
    even_w = (w_in_even, rg_conv_w, rg_conv_b, rg_w_a, rg_b_a, rg_w_i, rg_b_i, rg_lambda,
              gdn_conv_w, gdn_a_log, gdn_dt_bias, gdn_norm_g, w_out_even, ln_even_g, ln_even_b)
    odd_w = (w_in_odd, ml_b_i, ml_b_f, ml_norm_g, hg_lb, hg_norm_g, w_out_odd, ln_odd_g, ln_odd_b)
    return _forward(x_prompt, x_sample, states, even_w, odd_w)
```

```python
import functools

import jax
import jax.numpy as jnp
from jax import lax
from jax.experimental import pallas as pl
from jax.experimental.pallas import tpu as pltpu

F32 = jnp.float32
BF16 = jnp.bfloat16

D_MODEL = 1024
DEPTH = 4
CONV_W = 4
ALPHA = (2 * DEPTH) ** 0.25
EPS = 1e-5
RG_W = D_MODEL
RG_BLOCKS = 8
RG_BW = RG_W // RG_BLOCKS
RG_C = 8.0
GD_H = 8
GD_DK = D_MODEL // GD_H
GD_DV = D_MODEL // GD_H
GD_CONV = 3 * D_MODEL
ML_H = 4
ML_DK = D_MODEL // (2 * ML_H)
ML_DV = D_MODEL // ML_H
ML_QK = ML_H * ML_DK
HG_H = 8
HG_DK = D_MODEL // HG_H
HG_DV = D_MODEL // HG_H
HG_BLOCK = 16
CHUNK = 64
HG_UNROLL = 4
LOG2E = 1.4426950408889634

LANES = 128
SUBLANES = 8
CONV_PAD = SUBLANES
TIME_BLOCK = 256
VMEM_LIMIT_BYTES = 56 * 1024 * 1024

E_XA, E_GB, E_GA, E_Q, E_K, E_V, E_SM = 0, 1024, 2048, 3072, 4096, 5120, 6144
EVEN_COLS = E_SM + LANES
O_MO, O_HGATE, O_MG, O_MQ, O_MK, O_MV, O_HQ, O_HF, O_HI, O_SM = (
    0, 1024, 2048, 3072, 3584, 4096, 5120, 6144, 7168, 8192)
ODD_COLS = O_SM + LANES


def _dot(a, b):
    return jnp.dot(a.astype(BF16), b.astype(BF16), preferred_element_type=F32)


def _dot_nt(a, b):
    return lax.dot_general(a.astype(BF16), b.astype(BF16), (((1,), (1,)), ((), ())),
                           preferred_element_type=F32)


def _dot_tn(a, b):
    return lax.dot_general(a.astype(BF16), b.astype(BF16), (((0,), (0,)), ((), ())),
                           preferred_element_type=F32)


def _rows_from_cols(sel, m):
    hi = m.astype(BF16)
    r1 = m - hi.astype(F32)
    mid = r1.astype(BF16)
    lo = (r1 - mid.astype(F32)).astype(BF16)
    dn = (((1,), (1,)), ((), ()))
    out = lax.dot_general(sel, hi, dn, preferred_element_type=F32)
    out += lax.dot_general(sel, mid, dn, preferred_element_type=F32)
    out += lax.dot_general(sel, lo, dn, preferred_element_type=F32)
    return out


def _selector(lane0):
    r = lax.broadcasted_iota(jnp.int32, (SUBLANES, LANES), 0)
    c = lax.broadcasted_iota(jnp.int32, (SUBLANES, LANES), 1)
    return (c == r + lane0).astype(BF16)


def _sigmoid(x):
    return jax.nn.sigmoid(x)


def _silu(x):
    return x * jax.nn.sigmoid(x)


def _log1p_exp_neg_abs(x):
    return jnp.log(1.0 + jnp.exp(-jnp.abs(x)))


def _softplus(x):
    return jnp.maximum(x, 0.0) + _log1p_exp_neg_abs(x)


def _log_sigmoid(x):
    return jnp.minimum(x, 0.0) - _log1p_exp_neg_abs(x)


def _chunk_cumsum(x, chunk):
    n = x.shape[0]
    pos = lax.broadcasted_iota(jnp.int32, (n, 1), 0) % chunk
    d = 1
    while d < chunk:
        x = x + jnp.where(pos >= d, pltpu.roll(x, d, 0), 0.0)
        d *= 2
    return x


def _pair_tile_cumsum(x):
    n, w = x.shape
    x3 = x.reshape(n // SUBLANES, SUBLANES, w)
    sub = lax.broadcasted_iota(jnp.int32, (1, SUBLANES, 1), 1)
    d = 1
    while d < SUBLANES:
        x3 = x3 + jnp.where(sub >= d, pltpu.roll(x3, d, 1), 0.0)
        d *= 2
    x4 = x3.reshape(n // (2 * SUBLANES), 2, SUBLANES, w)
    first = x4[:, 0]
    second = x4[:, 1] + first[:, SUBLANES - 1:SUBLANES, :]
    return jnp.concatenate([first[:, None], second[:, None]], axis=1).reshape(n, w)


def _causal_conv(ext_ref, u, w_ref, tb):
    ext_ref[CONV_PAD:CONV_PAD + tb, :] = u
    base = CONV_PAD - (CONV_W - 1)
    nt8 = tb // SUBLANES
    wdt = u.shape[1]
    ext3 = ext_ref[...].reshape(nt8 + 1, SUBLANES, wdt)
    sub = lax.broadcasted_iota(jnp.int32, (1, SUBLANES, 1), 1)
    y = u.reshape(nt8, SUBLANES, wdt) * w_ref[CONV_W - 1:CONV_W, :]
    for s in range(1, CONV_W):
        r3 = pltpu.roll(ext3, s, 1)
        y = y + jnp.where(sub < s, r3[:nt8], r3[1:]) * w_ref[CONV_W - 1 - s:CONV_W - s, :]
    ext_ref[base:CONV_PAD, :] = ext_ref[base + tb:CONV_PAD + tb, :]
    return y.reshape(tb, wdt)


def _layer_norm_rows(x, g, b):
    mu = jnp.mean(x, axis=-1, keepdims=True)
    xc = x - mu
    var = jnp.mean(xc * xc, axis=-1, keepdims=True)
    return xc * lax.rsqrt(var + EPS) * g + b


def _tril_mask(c):
    r = lax.broadcasted_iota(jnp.int32, (c, c), 0)
    s = lax.broadcasted_iota(jnp.int32, (c, c), 1)
    return s <= r


def _even_kernel(*refs, tb, chunk, has_state):
    (x_ref, win_ref, cwa_ref, cba_ref, wai_ref, bai_ref, lam_ref, cwb_ref, gdp_ref, gng_ref,
     wout_ref, lng_ref, lnb_ref) = refs[:13]
    pos = 13
    if has_state:
        h0_ref, cva0_ref, s0_ref, cvb0_ref = refs[pos:pos + 4]
        pos += 4
    xo_ref, ho_ref, cvao_ref, so_ref, cvbo_ref = refs[pos:pos + 5]
    pos += 5
    (z_ref, y_ref, exta_ref, extb_ref, h_ref, s_ref, g_ref, beta_ref,
     p_ref, q_ref, qkm_ref) = refs[pos:pos + 11]

    t = pl.program_id(1)
    nt = pl.num_programs(1)
    base = CONV_PAD - (CONV_W - 1)

    @pl.when(t == 0)
    def _init():
        exta_ref[0:CONV_PAD, :] = jnp.zeros((CONV_PAD, RG_W), F32)
        extb_ref[0:CONV_PAD, :] = jnp.zeros((CONV_PAD, GD_CONV), F32)
        if has_state:
            h_ref[...] = h0_ref[...]
            exta_ref[base:CONV_PAD, :] = cva0_ref[...]
            extb_ref[base:CONV_PAD, :] = cvb0_ref[...]
            s_ref[...] = s0_ref[...]
        else:
            h_ref[...] = jnp.zeros_like(h_ref)
            s_ref[...] = jnp.zeros_like(s_ref)

    x = x_ref[...]
    xb = x.astype(BF16)
    ncol = 512
    for n0 in range(0, EVEN_COLS, ncol):
        n1 = min(n0 + ncol, EVEN_COLS)
        z_ref[:, n0:n1] = jnp.dot(xb, win_ref[:, n0:n1], preferred_element_type=F32)

    xa = _causal_conv(exta_ref, z_ref[:, E_XA:E_XA + RG_W], cwa_ref, tb) + cba_ref[...]
    gates = []
    for g in range(RG_BLOCKS):
        gates.append(_dot(xa[:, g * RG_BW:(g + 1) * RG_BW], wai_ref[g]))
    pre_r = jnp.concatenate([gt[:, :RG_BW] for gt in gates], axis=-1) + bai_ref[0:1, :]
    pre_i = jnp.concatenate([gt[:, RG_BW:] for gt in gates], axis=-1) + bai_ref[1:2, :]
    r = _sigmoid(pre_r)
    i = _sigmoid(pre_i)
    log_a = (-RG_C) * r * _softplus(-lam_ref[...])
    a = jnp.exp(log_a)
    th = jnp.tanh(log_a)
    u = jnp.sqrt((-2.0) * th / (1.0 - th)) * (i * xa)
    nt8 = tb // SUBLANES
    a3 = a.reshape(nt8, SUBLANES, RG_W)
    u3 = u.reshape(nt8, SUBLANES, RG_W)
    sub = lax.broadcasted_iota(jnp.int32, (1, SUBLANES, 1), 1)
    d = 1
    while d < SUBLANES:
        m = sub >= d
        u3 = a3 * jnp.where(m, pltpu.roll(u3, d, 1), 0.0) + u3
        a3 = a3 * jnp.where(m, pltpu.roll(a3, d, 1), 1.0)
        d *= 2
    hprev = h_ref[...]
    for r in range(nt8):
        ht = u3[r] + a3[r] * hprev
        hprev = ht[SUBLANES - 1:SUBLANES, :]
        rows8 = slice(r * SUBLANES, (r + 1) * SUBLANES)
        y_ref[rows8, 0:RG_W] = ht * _silu(z_ref[rows8, E_GA:E_GA + RG_W])
    h_ref[...] = hprev

    qkv = _silu(_causal_conv(extb_ref, z_ref[:, E_Q:E_Q + GD_CONV], cwb_ref, tb))
    z_ref[:, E_Q:E_Q + GD_CONV] = qkv
    sm = z_ref[:, E_SM:E_SM + LANES]
    beta_ref[...] = _sigmoid(sm)
    g_all = -jnp.exp(gdp_ref[0:1, :]) * _softplus(sm + gdp_ref[1:2, :])
    g_ref[...] = _chunk_cumsum(g_all, chunk)
    for h in range(GD_H):
        lq = slice(E_Q + h * GD_DK, E_Q + (h + 1) * GD_DK)
        lk = slice(E_K + h * GD_DK, E_K + (h + 1) * GD_DK)
        qh = z_ref[:, lq]
        z_ref[:, lq] = qh * (lax.rsqrt(jnp.sum(qh * qh, axis=-1, keepdims=True) + 1e-6) * (GD_DK ** -0.5))
        kh = z_ref[:, lk]
        z_ref[:, lk] = kh * lax.rsqrt(jnp.sum(kh * kh, axis=-1, keepdims=True) + 1e-6)

    gw = 2 * LANES
    hg = gw // chunk
    ng = GD_H // hg
    nc = tb // chunk
    lc = chunk.bit_length() - 1
    assert 1 << lc == chunk and hg * chunk == gw and ng * hg == GD_H

    def _iota(shape, dim):
        return lax.broadcasted_iota(jnp.int32, shape, dim)

    ri = _iota((chunk, gw), 0)
    cj = jnp.bitwise_and(_iota((chunk, gw), 1), chunk - 1)
    tril_cat = cj <= ri
    strict_cat = cj < ri
    eye_cat = (cj == ri).astype(F32)
    bdmask = (lax.shift_right_logical(_iota((gw, gw), 0), lc)
              == lax.shift_right_logical(_iota((gw, gw), 1), lc))
    kw = hg * GD_DK
    bdmask_k = (lax.shift_right_logical(_iota((gw, kw), 0), lc)
                == lax.shift_right_logical(_iota((gw, kw), 1), 7))
    ex = (_iota((LANES, 2 * GD_H * chunk), 0)
          == lax.shift_right_logical(_iota((LANES, 2 * GD_H * chunk), 1), lc)).astype(BF16)
    lane = _iota((1, LANES), 1)
    sel = _selector(GD_H)

    def block_diag(xcat):
        return jnp.where(bdmask, jnp.concatenate([xcat] * hg, axis=0), 0.0).astype(BF16)

    for c in range(nc):
        rows = slice(c * chunk, (c + 1) * chunk)
        gblk = g_ref[rows, :]
        bblk = beta_ref[rows, :]
        grow = _rows_from_cols(sel, gblk)
        gr_all = jnp.concatenate([grow[h:h + 1, :] for h in range(GD_H)], axis=-1)
        m = jnp.where(lane < GD_H, bblk, gblk)
        hi = m.astype(BF16)
        r1 = m - hi.astype(F32)
        mid = r1.astype(BF16)
        lo = (r1 - mid.astype(F32)).astype(BF16)
        ex3 = (jnp.dot(hi, ex, preferred_element_type=F32) + jnp.dot(mid, ex, preferred_element_type=F32)
               + jnp.dot(lo, ex, preferred_element_type=F32))
        for g in range(ng):
            item = c * ng + g
            bcat = ex3[:, g * gw:(g + 1) * gw]
            gcat = ex3[:, GD_H * chunk + g * gw:GD_H * chunk + (g + 1) * gw]
            dec = jnp.exp(jnp.where(tril_cat, gcat - gr_all[:, g * gw:(g + 1) * gw], -jnp.inf))
            k_g = z_ref[rows, E_K + g * kw:E_K + (g + 1) * kw]
            q_g = z_ref[rows, E_Q + g * kw:E_Q + (g + 1) * kw]
            kq = jnp.concatenate([k_g, q_g], axis=0)
            kbd = jnp.where(bdmask_k, jnp.concatenate([k_g] * hg, axis=0), 0.0)
            res = _dot_nt(kq, kbd)
            amat = jnp.where(strict_cat, bcat * res[:chunk] * dec, 0.0)
            qkm_ref[item] = res[chunk:] * dec
            p_ref[item] = eye_cat - amat
            q_ref[item] = amat

    for item in range(nc * ng):
        amat = q_ref[item]
        q_ref[item] = _dot(amat, block_diag(amat))
    span = 2
    while span < chunk:
        last = 2 * span >= chunk
        for item in range(nc * ng):
            p = p_ref[item]
            q = q_ref[item]
            qbd = block_diag(q)
            if last:
                p_ref[item] = p + _dot(p, qbd)
            else:
                pq = _dot(jnp.concatenate([p, q], axis=0), qbd)
                p_ref[item] = p + pq[:chunk]
                q_ref[item] = pq[chunk:]
        span *= 2

    for c in range(nc):
        rows = slice(c * chunk, (c + 1) * chunk)
        gblk = g_ref[rows, :]
        bblk = beta_ref[rows, :]
        first = []
        for h in range(GD_H):
            lq = slice(E_Q + h * GD_DK, E_Q + (h + 1) * GD_DK)
            lk = slice(E_K + h * GD_DK, E_K + (h + 1) * GD_DK)
            gc = gblk[:, GD_H + h:GD_H + h + 1]
            bc = bblk[:, h:h + 1]
            eg = jnp.exp(gc)
            kh = z_ref[rows, lk]
            s_old = s_ref[h]
            r1 = _dot(jnp.concatenate([kh * (bc * eg), z_ref[rows, lq] * eg], axis=0), s_old)
            first.append((gc, bc, kh, s_old, r1))
        second = []
        for h in range(GD_H):
            gc, bc, kh, s_old, r1 = first[h]
            item = c * ng + h // hg
            li = (h % hg) * chunk
            lv = slice(E_V + h * GD_DV, E_V + (h + 1) * GD_DV)
            dlt = _dot(p_ref[item, :, li:li + chunk], z_ref[rows, lv] * bc - r1[:chunk])
            second.append(dlt)
        for h in range(GD_H):
            gc, bc, kh, s_old, r1 = first[h]
            dlt = second[h]
            item = c * ng + h // hg
            li = (h % hg) * chunk
            o = r1[chunk:] + _dot(qkm_ref[item, :, li:li + chunk], dlt)
            gl = gc[chunk - 1:chunk, :]
            s_ref[h] = jnp.exp(gl) * s_old + _dot_tn(kh * jnp.exp(gl - gc), dlt)
            on = o * lax.rsqrt(jnp.mean(o * o, axis=-1, keepdims=True) + EPS) * gng_ref[...]
            lg = slice(E_GB + h * GD_DV, E_GB + (h + 1) * GD_DV)
            ly = slice(RG_W + h * GD_DV, RG_W + (h + 1) * GD_DV)
            y_ref[rows, ly] = on * _silu(z_ref[rows, lg])

    y = jnp.dot(y_ref[...].astype(BF16), wout_ref[...], preferred_element_type=F32)
    xo_ref[...] = _layer_norm_rows(ALPHA * x + y, lng_ref[...], lnb_ref[...])

    @pl.when(t == nt - 1)
    def _fin():
        ho_ref[...] = h_ref[...]
        cvao_ref[...] = exta_ref[base:CONV_PAD, :]
        cvbo_ref[...] = extb_ref[base:CONV_PAD, :]
        so_ref[...] = s_ref[...]


def _odd_kernel(*refs, tb, chunk, layer, has_state):
    (x_ref, win_ref, mlp_ref, mng_ref, hlb_ref, hng_ref, wout_ref, lng_ref, lnb_ref) = refs[:9]
    pos = 9
    if has_state:
        c0_ref, n0_ref, m0_ref, hs0_ref = refs[pos:pos + 4]
        pos += 4
    xo_ref, co_ref, no_ref, mo_ref, hso_ref = refs[pos:pos + 5]
    pos += 5
    z_ref, y_ref, c_ref, n_ref, m_ref, st_ref, gi_ref, a1_ref, kv_ref, loc_ref, oi_ref = refs[pos:pos + 11]

    t = pl.program_id(1)
    nt = pl.num_programs(1)
    hp = HG_H // 2

    @pl.when(t == 0)
    def _init():
        if has_state:
            c_ref[...] = c0_ref[...]
            n_ref[...] = n0_ref[...]
            m_ref[...] = m0_ref[...]
            for h in range(HG_H):
                st_ref[h // 2, :, (h % 2) * HG_DK:(h % 2 + 1) * HG_DK] = hs0_ref[h].T
        else:
            c_ref[...] = jnp.zeros_like(c_ref)
            n_ref[...] = jnp.zeros_like(n_ref)
            m_ref[...] = jnp.zeros_like(m_ref)
            st_ref[...] = jnp.zeros_like(st_ref)

    x = x_ref[...]
    xb = x.astype(BF16)
    ncol = 512
    for n0 in range(0, ODD_COLS, ncol):
        n1 = min(n0 + ncol, ODD_COLS)
        z_ref[:, n0:n1] = jnp.dot(xb, win_ref[:, n0:n1], preferred_element_type=F32)

    sm = z_ref[:, O_SM:O_SM + LANES]
    ig_all = sm + mlp_ref[0:1, :]
    lf_all = _log_sigmoid(sm + mlp_ref[1:2, :])
    b_all = _chunk_cumsum(lf_all, chunk)
    lane = lax.broadcasted_iota(jnp.int32, (1, LANES), 1)
    gi_ref[...] = jnp.where(lane < ML_H, ig_all, b_all)
    sel = _selector(0)
    tril = _tril_mask(chunk)
    nc = tb // chunk

    for c in range(nc):
        rows = slice(c * chunk, (c + 1) * chunk)
        gblk = gi_ref[rows, :]
        grow = _rows_from_cols(sel, gblk)
        for h in range(ML_H):
            item = c * ML_H + h
            qh = z_ref[rows, O_MQ + h * ML_DK:O_MQ + (h + 1) * ML_DK] * (ML_DK ** -0.5)
            kh = z_ref[rows, O_MK + h * ML_DK:O_MK + (h + 1) * ML_DK]
            vh = z_ref[rows, O_MV + h * ML_DV:O_MV + (h + 1) * ML_DV].astype(BF16)
            igc = gblk[:, h:h + 1]
            bc = gblk[:, ML_H + h:ML_H + h + 1]
            dm = jnp.where(tril, bc - grow[ML_H + h:ML_H + h + 1, :] + grow[h:h + 1, :], -jnp.inf)
            md = jnp.max(dm, axis=-1, keepdims=True)
            qkl = _dot_nt(qh, kh) * jnp.exp(dm - md)
            a1_ref[item] = _dot(qkl, vh)
            kwl = kh * jnp.exp(bc[chunk - 1:chunk, :] - bc + igc - md[chunk - 1:chunk, :])
            kv_ref[item] = _dot_tn(kwl, vh)
            loc_ref[item, 0:chunk, 0:1] = md
            loc_ref[item, 0:chunk, 1:2] = jnp.sum(qkl, axis=-1, keepdims=True)
            loc_ref[item, chunk:chunk + 1, :] = jnp.sum(kwl, axis=0, keepdims=True)

    for c in range(nc):
        rows = slice(c * chunk, (c + 1) * chunk)
        gblk = gi_ref[rows, :]
        for h in range(ML_H):
            item = c * ML_H + h
            qh = z_ref[rows, O_MQ + h * ML_DK:O_MQ + (h + 1) * ML_DK] * (ML_DK ** -0.5)
            bc = gblk[:, ML_H + h:ML_H + h + 1]
            md = loc_ref[item, 0:chunk, 0:1]
            rs = loc_ref[item, 0:chunk, 1:2]
            kn = loc_ref[item, chunk:chunk + 1, :]
            m_old = m_ref[h:h + 1, 0:1]
            c_old = c_ref[h]
            n_old = n_ref[h:h + 1, :]
            inter = bc + m_old
            mt = jnp.maximum(inter, md)
            sc = jnp.exp(inter - mt)
            sd = jnp.exp(md - mt)
            num = sc * _dot(qh, c_old) + sd * a1_ref[item]
            den = sc * jnp.sum(qh * n_old, axis=-1, keepdims=True) + sd * rs
            hh = num / jnp.maximum(jnp.abs(den), jnp.exp(-mt))
            ml = mt[chunk - 1:chunk, :]
            sl = jnp.exp(inter[chunk - 1:chunk, :] - ml)
            sdl = sd[chunk - 1:chunk, :]
            c_ref[h] = sl * c_old + sdl * kv_ref[item]
            n_ref[h:h + 1, :] = sl * n_old + sdl * kn
            m_ref[h:h + 1, :] = jnp.broadcast_to(ml, (1, LANES))
            mu = jnp.mean(hh, axis=-1, keepdims=True)
            hc = hh - mu
            var = jnp.mean(hc * hc, axis=-1, keepdims=True)
            hn = hc * lax.rsqrt(var + EPS) * mng_ref[...]
            lo = slice(O_MO + h * ML_DV, O_MO + (h + 1) * ML_DV)
            lg = slice(O_MG + h * ML_DV, O_MG + (h + 1) * ML_DV)
            y_ref[rows, h * ML_DV:(h + 1) * ML_DV] = _sigmoid(z_ref[rows, lo]) * hn * _silu(z_ref[rows, lg])

    lbp = hlb_ref[...]
    e = jnp.exp(lbp - jnp.max(lbp, axis=0, keepdims=True))
    smx = e / jnp.sum(e, axis=0, keepdims=True)
    lb = smx[0:1, :] * 0.0
    for jj in range(1, layer + 1):
        lb = lb + smx[jj:jj + 1, :]
    hf = z_ref[:, O_HF:O_HF + D_MODEL]
    la = jnp.log(lb)
    lbv = jnp.log1p(-lb) + _log_sigmoid(hf)
    logf = jnp.maximum(la, lbv) + _log1p_exp_neg_abs(la - lbv)
    assert HG_BLOCK == 2 * SUBLANES
    g2 = _pair_tile_cumsum(logf) * LOG2E
    z_ref[:, O_MG:O_MG + D_MODEL] = g2
    z_ref[:, O_HF:O_HF + D_MODEL] = g2 - jnp.log((1.0 - lb) * _sigmoid(-hf)) * LOG2E
    z_ref[:, O_HQ:O_HQ + D_MODEL] = _silu(z_ref[:, O_HQ:O_HQ + D_MODEL]) * (HG_DK ** -0.5)
    half = HG_BLOCK // 2
    rowi = lax.broadcasted_iota(jnp.int32, (HG_BLOCK, 1), 0)
    kcat = HG_BLOCK * HG_DK
    sel_j = (lax.broadcasted_iota(jnp.int32, (HG_BLOCK, kcat), 0)
             == lax.shift_right_logical(lax.broadcasted_iota(jnp.int32, (HG_BLOCK, kcat), 1), 7)).astype(BF16)
    pair_mask = (lax.shift_right_logical(lax.broadcasted_iota(jnp.int32, (2 * HG_BLOCK, 2 * HG_DK), 0), 4)
                 == lax.shift_right_logical(lax.broadcasted_iota(jnp.int32, (2 * HG_BLOCK, 2 * HG_DK), 1), 7))
    zero_half = jnp.zeros((half, HG_DK), F32)
    zero_rows = jnp.zeros((LANES - HG_BLOCK, HG_H * HG_BLOCK), F32)

    ub = min(HG_UNROLL, tb // HG_BLOCK)

    def intra_group(r0):
        pcats = []
        for u in range(ub):
            rows = pl.ds(r0 + u * HG_BLOCK, HG_BLOCK)
            gb = z_ref[rows, O_MG:O_MG + D_MODEL]
            hb = z_ref[rows, O_HF:O_HF + D_MODEL]
            qb = z_ref[rows, O_HQ:O_HQ + D_MODEL]
            pieces = []
            for j in range(HG_BLOCK):
                lo = 0 if j < half else half
                pj = qb[lo:] * jnp.exp2(jnp.where(rowi[lo:] >= j, gb[lo:] - hb[j:j + 1, :], -jnp.inf))
                tiles = []
                for h in range(HG_H):
                    if lo:
                        tiles.append(zero_half)
                    tiles.append(pj[:, h * HG_DK:(h + 1) * HG_DK])
                pieces.append(jnp.concatenate(tiles, axis=0))
            pcats.append(jnp.concatenate(pieces, axis=1).astype(BF16))
        atts = []
        for u in range(ub):
            att_t = lax.dot_general(sel_j, pcats[u], (((1,), (1,)), ((), ())),
                                    preferred_element_type=F32)
            atts.append(jnp.concatenate([att_t, zero_rows], axis=0).T)
        for u in range(ub):
            rows = pl.ds(r0 + u * HG_BLOCK, HG_BLOCK)
            vb = z_ref[rows, O_HI:O_HI + D_MODEL]
            for h in range(HG_H):
                att_h = atts[u][h * HG_BLOCK:(h + 1) * HG_BLOCK, 0:HG_BLOCK]
                oi_ref[rows, h * HG_DV:(h + 1) * HG_DV] = _dot(att_h, vb[:, h * HG_DV:(h + 1) * HG_DV])

    def state_group(r0):
        for u in range(ub):
            rows = pl.ds(r0 + u * HG_BLOCK, HG_BLOCK)
            gb = z_ref[rows, O_MG:O_MG + D_MODEL]
            gl = gb[HG_BLOCK - 1:HG_BLOCK, :]
            egl = jnp.exp2(gl)
            qe = z_ref[rows, O_HQ:O_HQ + D_MODEL] * jnp.exp2(gb)
            ke = jnp.exp2(gl - z_ref[rows, O_HF:O_HF + D_MODEL])
            vb = z_ref[rows, O_HI:O_HI + D_MODEL]
            for h in range(HG_H):
                ly = slice(D_MODEL + h * HG_DV, D_MODEL + (h + 1) * HG_DV)
                st_old = st_ref[h // 2, :, (h % 2) * HG_DK:(h % 2 + 1) * HG_DK]
                o = oi_ref[rows, h * HG_DV:(h + 1) * HG_DV] + _dot_nt(qe[:, h * HG_DK:(h + 1) * HG_DK], st_old)
                on = o * lax.rsqrt(jnp.mean(o * o, axis=-1, keepdims=True) + EPS) * hng_ref[...]
                lgt = slice(O_HGATE + h * HG_DV, O_HGATE + (h + 1) * HG_DV)
                y_ref[rows, ly] = on * _silu(z_ref[rows, lgt])
            for p in range(hp):
                lp = slice(2 * p * HG_DK, (2 * p + 2) * HG_DK)
                vst = jnp.concatenate([vb[:, 2 * p * HG_DV:(2 * p + 1) * HG_DV],
                                       vb[:, (2 * p + 1) * HG_DV:(2 * p + 2) * HG_DV]], axis=0)
                kep = ke[:, lp]
                rhs = jnp.where(pair_mask, jnp.concatenate([kep, kep], axis=0), 0.0)
                st_ref[p] = egl[:, lp] * st_ref[p] + _dot_tn(vst, rhs)

    grp = ub * HG_BLOCK
    ngrp = tb // grp
    intra_group(0)

    def pipe_body(gi, carry):
        r0 = pl.multiple_of(gi * grp, grp)
        state_group(r0)
        intra_group(r0 + grp)
        return carry

    lax.fori_loop(0, ngrp - 1, pipe_body, 0)
    state_group((ngrp - 1) * grp)

    y = jnp.dot(y_ref[...].astype(BF16), wout_ref[...], preferred_element_type=F32)
    xo_ref[...] = _layer_norm_rows(ALPHA * x + y, lng_ref[...], lnb_ref[...])

    @pl.when(t == nt - 1)
    def _fin():
        co_ref[...] = c_ref[...]
        no_ref[...] = n_ref[...]
        mo_ref[...] = m_ref[...]
        for h in range(HG_H):
            hso_ref[h] = st_ref[h // 2, :, (h % 2) * HG_DK:(h % 2 + 1) * HG_DK].T


def _const_spec(shape):
    nd = len(shape)
    return pl.BlockSpec(shape, lambda b, t: (0,) * nd, pipeline_mode=pl.Buffered(1))


def _batch_spec(shape):
    nd = len(shape)
    return pl.BlockSpec((None,) + tuple(shape), lambda b, t: (b,) + (0,) * nd)


def _time_block(T, tb_max):
    tb = min(T, tb_max)
    while T % tb:
        tb -= 1
    return tb


def _chunk_len(tb, max_blk):
    c = min(max_blk, tb)
    while tb % c:
        c -= 1
    return c


def _even_layer(x, params, state, tb_max):
    B, T, D = x.shape
    tb = _time_block(T, tb_max)
    chunk = _chunk_len(tb, CHUNK)
    has_state = state is not None
    (win, cwa, cba, wai, bai, lam, cwb, gdp, gng, wout, lng, lnb) = params
    consts = [win, cwa, cba, wai, bai, lam, cwb, gdp, gng, wout, lng, lnb]
    in_specs = [pl.BlockSpec((None, tb, D), lambda b, t: (b, t, 0))]
    in_specs += [_const_spec(c.shape) for c in consts]
    args = [x] + consts
    state_shapes = [(1, RG_W), (CONV_W - 1, RG_W), (GD_H, GD_DK, GD_DV), (CONV_W - 1, GD_CONV)]
    if has_state:
        h0, cva0, s0, cvb0 = state
        args += [h0.reshape(B, 1, RG_W), cva0, s0, cvb0]
        in_specs += [_batch_spec(s) for s in state_shapes]
    out_shape = [jax.ShapeDtypeStruct((B, T, D), x.dtype)]
    out_shape += [jax.ShapeDtypeStruct((B,) + s, x.dtype) for s in state_shapes]
    out_specs = [pl.BlockSpec((None, tb, D), lambda b, t: (b, t, 0))]
    out_specs += [_batch_spec(s) for s in state_shapes]
    scratch = [
        pltpu.VMEM((tb, EVEN_COLS), F32),
        pltpu.VMEM((tb, 2 * D_MODEL), F32),
        pltpu.VMEM((CONV_PAD + tb, RG_W), F32),
        pltpu.VMEM((CONV_PAD + tb, GD_CONV), F32),
        pltpu.VMEM((1, RG_W), F32),
        pltpu.VMEM((GD_H, GD_DK, GD_DV), F32),
        pltpu.VMEM((tb, LANES), F32),
        pltpu.VMEM((tb, LANES), F32),
    ]
    n_items = (tb // chunk) * (GD_H * chunk // (2 * LANES))
    scratch += [pltpu.VMEM((n_items, chunk, 2 * LANES), F32) for _ in range(3)]
    outs = pl.pallas_call(
        functools.partial(_even_kernel, tb=tb, chunk=chunk, has_state=has_state),
        grid=(B, T // tb),
        in_specs=in_specs,
        out_specs=out_specs,
        out_shape=out_shape,
        scratch_shapes=scratch,
        compiler_params=pltpu.CompilerParams(
            dimension_semantics=("arbitrary", "arbitrary"), vmem_limit_bytes=VMEM_LIMIT_BYTES),
        name="even_layer_state" if has_state else "even_layer",
    )(*args)
    xo, ho, cvao, so, cvbo = outs
    return xo, ho.reshape(B, RG_W), cvao, so, cvbo


def _odd_layer(x, params, layer, state, tb_max):
    B, T, D = x.shape
    tb = _time_block(T, tb_max)
    chunk = _chunk_len(tb, CHUNK)
    has_state = state is not None
    (win, mlp, mng, hlb, hng, wout, lng, lnb) = params
    consts = [win, mlp, mng, hlb, hng, wout, lng, lnb]
    in_specs = [pl.BlockSpec((None, tb, D), lambda b, t: (b, t, 0))]
    in_specs += [_const_spec(c.shape) for c in consts]
    args = [x] + consts
    state_shapes = [(ML_H, ML_DK, ML_DV), (SUBLANES, ML_DK), (SUBLANES, LANES), (HG_H, HG_DK, HG_DV)]
    if has_state:
        c0, n0, m0, hs0 = state
        n0p = jnp.zeros((B, SUBLANES, ML_DK), F32).at[:, :ML_H, :].set(n0)
        m0p = jnp.zeros((B, SUBLANES, LANES), F32).at[:, :ML_H, :].set(
            jnp.broadcast_to(m0[:, :, None], (B, ML_H, LANES)))
        args += [c0, n0p, m0p, hs0]
        in_specs += [_batch_spec(s) for s in state_shapes]
    out_shape = [jax.ShapeDtypeStruct((B, T, D), x.dtype)]
    out_shape += [jax.ShapeDtypeStruct((B,) + s, x.dtype) for s in state_shapes]
    out_specs = [pl.BlockSpec((None, tb, D), lambda b, t: (b, t, 0))]
    out_specs += [_batch_spec(s) for s in state_shapes]
    scratch = [
        pltpu.VMEM((tb, ODD_COLS), F32),
        pltpu.VMEM((tb, 2 * D_MODEL), F32),
        pltpu.VMEM((ML_H, ML_DK, ML_DV), F32),
        pltpu.VMEM((SUBLANES, ML_DK), F32),
        pltpu.VMEM((SUBLANES, LANES), F32),
        pltpu.VMEM((HG_H // 2, HG_DV, 2 * HG_DK), F32),
        pltpu.VMEM((tb, LANES), F32),
        pltpu.VMEM(((tb // chunk) * ML_H, chunk, ML_DV), F32),
        pltpu.VMEM(((tb // chunk) * ML_H, ML_DK, ML_DV), F32),
        pltpu.VMEM(((tb // chunk) * ML_H, chunk + SUBLANES, LANES), F32),
        pltpu.VMEM((tb, D_MODEL), F32),
    ]
    outs = pl.pallas_call(
        functools.partial(_odd_kernel, tb=tb, chunk=chunk, layer=layer, has_state=has_state),
        grid=(B, T // tb),
        in_specs=in_specs,
        out_specs=out_specs,
        out_shape=out_shape,
        scratch_shapes=scratch,
        compiler_params=pltpu.CompilerParams(
            dimension_semantics=("arbitrary", "arbitrary"), vmem_limit_bytes=VMEM_LIMIT_BYTES),
        name="odd_layer_state" if has_state else "odd_layer",
    )(*args)
    xo, co, no, mo, hso = outs
    return xo, co, no[:, :ML_H, :], mo[:, :ML_H, 0], hso


def _pad_lanes(cols):
    return jnp.pad(cols, ((0, 0), (0, LANES - cols.shape[1])))


def _even_params(j, w_in_even, rg_conv_w, rg_conv_b, rg_w_a, rg_b_a, rg_w_i, rg_b_i, rg_lambda,
                 gdn_conv_w, gdn_a_log, gdn_dt_bias, gdn_norm_g, w_out_even, ln_even_g, ln_even_b):
    w = w_in_even[j]
    xa, ga, qb, kb, vb = (w[:, i * 1024:(i + 1) * 1024] for i in range(5))
    small = w[:, 5120:5136]
    gb = w[:, 5136:6160]
    win = jnp.concatenate([xa, gb, ga, qb, kb, vb, _pad_lanes(small)], axis=1).astype(BF16)
    wai = jnp.concatenate([rg_w_a[j], rg_w_i[j]], axis=-1).astype(BF16)
    bai = jnp.stack([rg_b_a[j], rg_b_i[j]]).astype(F32)
    gdp = jnp.zeros((2, LANES), F32)
    gdp = gdp.at[0, GD_H:2 * GD_H].set(gdn_a_log[j]).at[1, GD_H:2 * GD_H].set(gdn_dt_bias[j])
    return (win, rg_conv_w[j], rg_conv_b[j][None, :], wai, bai, rg_lambda[j][None, :],
            gdn_conv_w[j], gdp, gdn_norm_g[j][None, :], w_out_even[j].astype(BF16),
            ln_even_g[j][None, :], ln_even_b[j][None, :])


def _odd_params(j, w_in_odd, ml_b_i, ml_b_f, ml_norm_g, hg_lb, hg_norm_g, w_out_odd, ln_odd_g, ln_odd_b):
    w = w_in_odd[j]
    mq, mk, mv = w[:, 0:512], w[:, 512:1024], w[:, 1024:2048]
    small = w[:, 2048:2056]
    mo, mg, hq, hf, hi, hgate = (w[:, 2056 + i * 1024:2056 + (i + 1) * 1024] for i in range(6))
    win = jnp.concatenate([mo, hgate, mg, mq, mk, mv, hq, hf, hi, _pad_lanes(small)], axis=1).astype(BF16)
    mlp = jnp.zeros((2, LANES), F32)
    mlp = mlp.at[0, 0:ML_H].set(ml_b_i[j]).at[1, ML_H:2 * ML_H].set(ml_b_f[j])
    return (win, mlp, ml_norm_g[j][None, :], hg_lb.astype(F32), hg_norm_g[j][None, :],
            w_out_odd[j].astype(BF16), ln_odd_g[j][None, :], ln_odd_b[j][None, :])


def _run(x, states, even_p, odd_p, tb_max):
    rg_h, rg_cv, gd_s, gd_cv, ml_c, ml_n, ml_m, hg_s = ([] for _ in range(8))
    for l in range(DEPTH):
        j = l // 2
        if l % 2 == 0:
            st = None if states is None else (states[0][j], states[1][j], states[2][j], states[3][j])
            x, h, cva, s, cvb = _even_layer(x, even_p[j], st, tb_max)
            rg_h.append(h); rg_cv.append(cva); gd_s.append(s); gd_cv.append(cvb)
        else:
            st = None if states is None else (states[4][j], states[5][j], states[6][j], states[7][j])
            x, c, n, m, s = _odd_layer(x, odd_p[j], j, st, tb_max)
            ml_c.append(c); ml_n.append(n); ml_m.append(m); hg_s.append(s)
    return (x, jnp.stack(rg_h), jnp.stack(rg_cv), jnp.stack(gd_s), jnp.stack(gd_cv),
            jnp.stack(ml_c), jnp.stack(ml_n), jnp.stack(ml_m), jnp.stack(hg_s))


def _forward(x_prompt, x_sample, states, even_w, odd_w, tb_max=TIME_BLOCK):
    n_even = even_w[0].shape[0]
    n_odd = odd_w[0].shape[0]
    even_p = [_even_params(j, *even_w) for j in range(n_even)]
    odd_p = [_odd_params(j, *odd_w) for j in range(n_odd)]
    p = _run(x_prompt, None, even_p, odd_p, tb_max)
    s = _run(x_sample, states, even_p, odd_p, tb_max)
    return (p[0], s[0]) + p[1:] + s[1:]


def kernel(x_prompt, x_sample, state_rglru_h, state_rglru_conv, state_gdn_S, state_gdn_conv, state_mlstm_C, state_mlstm_n, state_mlstm_m, state_hgrn_S, w_in_even, rg_conv_w, rg_conv_b, rg_w_a, rg_b_a, rg_w_i, rg_b_i, rg_lambda, gdn_conv_w, gdn_a_log, gdn_dt_bias, gdn_norm_g, w_out_even, ln_even_g, ln_even_b, w_in_odd, ml_b_i, ml_b_f, ml_norm_g, hg_lb, hg_norm_g, w_out_odd, ln_odd_g, ln_odd_b):
    states = (state_rglru_h, state_rglru_conv, state_gdn_S, state_gdn_conv,
              state_mlstm_C, state_mlstm_n, state_mlstm_m, state_hgrn_S)
    even_w = (w_in_even, rg_conv_w, rg_conv_b, rg_w_a, rg_b_a, rg_w_i, rg_b_i, rg_lambda,
              gdn_conv_w, gdn_a_log, gdn_dt_bias, gdn_norm_g, w_out_even, ln_even_g, ln_even_b)
    odd_w = (w_in_odd, ml_b_i, ml_b_f, ml_norm_g, hg_lb, hg_norm_g, w_out_odd, ln_odd_g, ln_odd_b)
    return _forward(x_prompt, x_sample, states, even_w, odd_w)
```

```python
import functools

import jax
import jax.numpy as jnp
from jax import lax
from jax.experimental import pallas as pl
from jax.experimental.pallas import tpu as pltpu

F32 = jnp.float32
BF16 = jnp.bfloat16

D_MODEL = 1024
DEPTH = 4
CONV_W = 4
ALPHA = (2 * DEPTH) ** 0.25
EPS = 1e-5
RG_W = D_MODEL
RG_BLOCKS = 8
RG_BW = RG_W // RG_BLOCKS
RG_C = 8.0
GD_H = 8
GD_DK = D_MODEL // GD_H
GD_DV = D_MODEL // GD_H
GD_CONV = 3 * D_MODEL
ML_H = 4
ML_DK = D_MODEL // (2 * ML_H)
ML_DV = D_MODEL // ML_H
ML_QK = ML_H * ML_DK
HG_H = 8
HG_DK = D_MODEL // HG_H
HG_DV = D_MODEL // HG_H
HG_BLOCK = 16
CHUNK = 64
HG_UNROLL = 4
LOG2E = 1.4426950408889634

LANES = 128
SUBLANES = 8
CONV_PAD = SUBLANES
TIME_BLOCK = 256
VMEM_LIMIT_BYTES = 56 * 1024 * 1024

E_XA, E_GB, E_GA, E_Q, E_K, E_V, E_SM = 0, 1024, 2048, 3072, 4096, 5120, 6144
EVEN_COLS = E_SM + LANES
O_MO, O_HGATE, O_MG, O_MQ, O_MK, O_MV, O_HQ, O_HF, O_HI, O_SM = (
    0, 1024, 2048, 3072, 3584, 4096, 5120, 6144, 7168, 8192)
ODD_COLS = O_SM + LANES


def _dot(a, b):
    return jnp.dot(a.astype(BF16), b.astype(BF16), preferred_element_type=F32)


def _dot_nt(a, b):
    return lax.dot_general(a.astype(BF16), b.astype(BF16), (((1,), (1,)), ((), ())),
                           preferred_element_type=F32)


def _dot_tn(a, b):
    return lax.dot_general(a.astype(BF16), b.astype(BF16), (((0,), (0,)), ((), ())),
                           preferred_element_type=F32)


def _rows_from_cols(sel, m):
    hi = m.astype(BF16)
    lo = (m - hi.astype(F32)).astype(BF16)
    dn = (((1,), (1,)), ((), ()))
    out = lax.dot_general(sel, hi, dn, preferred_element_type=F32)
    out += lax.dot_general(sel, lo, dn, preferred_element_type=F32)
    return out


def _selector(lane0):
    r = lax.broadcasted_iota(jnp.int32, (SUBLANES, LANES), 0)
    c = lax.broadcasted_iota(jnp.int32, (SUBLANES, LANES), 1)
    return (c == r + lane0).astype(BF16)


def _sigmoid(x):
    return jax.nn.sigmoid(x)


def _silu(x):
    return x * jax.nn.sigmoid(x)


def _log1p_exp_neg_abs(x):
    return jnp.log(1.0 + jnp.exp(-jnp.abs(x)))


def _softplus(x):
    return jnp.maximum(x, 0.0) + _log1p_exp_neg_abs(x)


def _log_sigmoid(x):
    return jnp.minimum(x, 0.0) - _log1p_exp_neg_abs(x)


def _chunk_cumsum(x, chunk):
    n = x.shape[0]
    pos = lax.broadcasted_iota(jnp.int32, (n, 1), 0) % chunk
    d = 1
    while d < chunk:
        x = x + jnp.where(pos >= d, pltpu.roll(x, d, 0), 0.0)
        d *= 2
    return x


def _pair_tile_cumsum(x):
    n, w = x.shape
    x3 = x.reshape(n // SUBLANES, SUBLANES, w)
    sub = lax.broadcasted_iota(jnp.int32, (1, SUBLANES, 1), 1)
    d = 1
    while d < SUBLANES:
        x3 = x3 + jnp.where(sub >= d, pltpu.roll(x3, d, 1), 0.0)
        d *= 2
    x4 = x3.reshape(n // (2 * SUBLANES), 2, SUBLANES, w)
    first = x4[:, 0]
    second = x4[:, 1] + first[:, SUBLANES - 1:SUBLANES, :]
    return jnp.concatenate([first[:, None], second[:, None]], axis=1).reshape(n, w)


def _causal_conv(ext_ref, u, w_ref, tb):
    ext_ref[CONV_PAD:CONV_PAD + tb, :] = u
    base = CONV_PAD - (CONV_W - 1)
    nt8 = tb // SUBLANES
    wdt = u.shape[1]
    ext3 = ext_ref[...].reshape(nt8 + 1, SUBLANES, wdt)
    sub = lax.broadcasted_iota(jnp.int32, (1, SUBLANES, 1), 1)
    y = u.reshape(nt8, SUBLANES, wdt) * w_ref[CONV_W - 1:CONV_W, :]
    for s in range(1, CONV_W):
        r3 = pltpu.roll(ext3, s, 1)
        y = y + jnp.where(sub < s, r3[:nt8], r3[1:]) * w_ref[CONV_W - 1 - s:CONV_W - s, :]
    ext_ref[base:CONV_PAD, :] = ext_ref[base + tb:CONV_PAD + tb, :]
    return y.reshape(tb, wdt)


def _layer_norm_rows(x, g, b):
    mu = jnp.mean(x, axis=-1, keepdims=True)
    xc = x - mu
    var = jnp.mean(xc * xc, axis=-1, keepdims=True)
    return xc * lax.rsqrt(var + EPS) * g + b


def _tril_mask(c):
    r = lax.broadcasted_iota(jnp.int32, (c, c), 0)
    s = lax.broadcasted_iota(jnp.int32, (c, c), 1)
    return s <= r


def _even_kernel(*refs, tb, chunk, has_state):
    (x_ref, win_ref, cwa_ref, cba_ref, wai_ref, bai_ref, lam_ref, cwb_ref, gdp_ref, gng_ref,
     wout_ref, lng_ref, lnb_ref) = refs[:13]
    pos = 13
    if has_state:
        h0_ref, cva0_ref, s0_ref, cvb0_ref = refs[pos:pos + 4]
        pos += 4
    xo_ref, ho_ref, cvao_ref, so_ref, cvbo_ref = refs[pos:pos + 5]
    pos += 5
    (z_ref, y_ref, exta_ref, extb_ref, h_ref, s_ref, g_ref, beta_ref,
     p_ref, q_ref, qkm_ref) = refs[pos:pos + 11]

    t = pl.program_id(1)
    nt = pl.num_programs(1)
    base = CONV_PAD - (CONV_W - 1)

    @pl.when(t == 0)
    def _init():
        exta_ref[0:CONV_PAD, :] = jnp.zeros((CONV_PAD, RG_W), F32)
        extb_ref[0:CONV_PAD, :] = jnp.zeros((CONV_PAD, GD_CONV), F32)
        if has_state:
            h_ref[...] = h0_ref[...]
            exta_ref[base:CONV_PAD, :] = cva0_ref[...]
            extb_ref[base:CONV_PAD, :] = cvb0_ref[...]
            s_ref[...] = s0_ref[...]
        else:
            h_ref[...] = jnp.zeros_like(h_ref)
            s_ref[...] = jnp.zeros_like(s_ref)

    x = x_ref[...]
    xb = x.astype(BF16)
    ncol = 512
    for n0 in range(0, EVEN_COLS, ncol):
        n1 = min(n0 + ncol, EVEN_COLS)
        z_ref[:, n0:n1] = jnp.dot(xb, win_ref[:, n0:n1], preferred_element_type=F32)

    xa = _causal_conv(exta_ref, z_ref[:, E_XA:E_XA + RG_W], cwa_ref, tb) + cba_ref[...]
    gates = []
    for g in range(RG_BLOCKS):
        gates.append(_dot(xa[:, g * RG_BW:(g + 1) * RG_BW], wai_ref[g]))
    pre_r = jnp.concatenate([gt[:, :RG_BW] for gt in gates], axis=-1) + bai_ref[0:1, :]
    pre_i = jnp.concatenate([gt[:, RG_BW:] for gt in gates], axis=-1) + bai_ref[1:2, :]
    r = _sigmoid(pre_r)
    i = _sigmoid(pre_i)
    log_a = (-RG_C) * r * _softplus(-lam_ref[...])
    a = jnp.exp(log_a)
    th = jnp.tanh(log_a)
    u = jnp.sqrt((-2.0) * th / (1.0 - th)) * (i * xa)
    nt8 = tb // SUBLANES
    a3 = a.reshape(nt8, SUBLANES, RG_W)
    u3 = u.reshape(nt8, SUBLANES, RG_W)
    sub = lax.broadcasted_iota(jnp.int32, (1, SUBLANES, 1), 1)
    d = 1
    while d < SUBLANES:
        m = sub >= d
        u3 = a3 * jnp.where(m, pltpu.roll(u3, d, 1), 0.0) + u3
        a3 = a3 * jnp.where(m, pltpu.roll(a3, d, 1), 1.0)
        d *= 2
    hprev = h_ref[...]
    for r in range(nt8):
        ht = u3[r] + a3[r] * hprev
        hprev = ht[SUBLANES - 1:SUBLANES, :]
        rows8 = slice(r * SUBLANES, (r + 1) * SUBLANES)
        y_ref[rows8, 0:RG_W] = ht * _silu(z_ref[rows8, E_GA:E_GA + RG_W])
    h_ref[...] = hprev

    qkv = _silu(_causal_conv(extb_ref, z_ref[:, E_Q:E_Q + GD_CONV], cwb_ref, tb))
    z_ref[:, E_Q:E_Q + GD_CONV] = qkv
    sm = z_ref[:, E_SM:E_SM + LANES]
    beta_ref[...] = _sigmoid(sm)
    g_all = -jnp.exp(gdp_ref[0:1, :]) * _softplus(sm + gdp_ref[1:2, :])
    g_ref[...] = _chunk_cumsum(g_all, chunk)
    for h in range(GD_H):
        lq = slice(E_Q + h * GD_DK, E_Q + (h + 1) * GD_DK)
        lk = slice(E_K + h * GD_DK, E_K + (h + 1) * GD_DK)
        qh = z_ref[:, lq]
        z_ref[:, lq] = qh * (lax.rsqrt(jnp.sum(qh * qh, axis=-1, keepdims=True) + 1e-6) * (GD_DK ** -0.5))
        kh = z_ref[:, lk]
        z_ref[:, lk] = kh * lax.rsqrt(jnp.sum(kh * kh, axis=-1, keepdims=True) + 1e-6)

    gw = 2 * LANES
    hg = gw // chunk
    ng = GD_H // hg
    nc = tb // chunk
    lc = chunk.bit_length() - 1
    assert 1 << lc == chunk and hg * chunk == gw and ng * hg == GD_H

    def _iota(shape, dim):
        return lax.broadcasted_iota(jnp.int32, shape, dim)

    ri = _iota((chunk, gw), 0)
    cj = jnp.bitwise_and(_iota((chunk, gw), 1), chunk - 1)
    tril_cat = cj <= ri
    strict_cat = cj < ri
    eye_cat = (cj == ri).astype(F32)
    bdmask = (lax.shift_right_logical(_iota((gw, gw), 0), lc)
              == lax.shift_right_logical(_iota((gw, gw), 1), lc))
    kw = hg * GD_DK
    bdmask_k = (lax.shift_right_logical(_iota((gw, kw), 0), lc)
                == lax.shift_right_logical(_iota((gw, kw), 1), 7))
    ex = (_iota((LANES, 2 * GD_H * chunk), 0)
          == lax.shift_right_logical(_iota((LANES, 2 * GD_H * chunk), 1), lc)).astype(BF16)
    lane = _iota((1, LANES), 1)
    sel = _selector(GD_H)

    def block_diag(xcat):
        return jnp.where(bdmask, jnp.concatenate([xcat] * hg, axis=0), 0.0).astype(BF16)

    for c in range(nc):
        rows = slice(c * chunk, (c + 1) * chunk)
        gblk = g_ref[rows, :]
        bblk = beta_ref[rows, :]
        grow = _rows_from_cols(sel, gblk)
        gr_all = jnp.concatenate([grow[h:h + 1, :] for h in range(GD_H)], axis=-1)
        m = jnp.where(lane < GD_H, bblk, gblk)
        hi = m.astype(BF16)
        lo = (m - hi.astype(F32)).astype(BF16)
        ex3 = jnp.dot(hi, ex, preferred_element_type=F32) + jnp.dot(lo, ex, preferred_element_type=F32)
        for g in range(ng):
            item = c * ng + g
            bcat = ex3[:, g * gw:(g + 1) * gw]
            gcat = ex3[:, GD_H * chunk + g * gw:GD_H * chunk + (g + 1) * gw]
            dec = jnp.exp(jnp.where(tril_cat, gcat - gr_all[:, g * gw:(g + 1) * gw], -jnp.inf))
            k_g = z_ref[rows, E_K + g * kw:E_K + (g + 1) * kw]
            q_g = z_ref[rows, E_Q + g * kw:E_Q + (g + 1) * kw]
            kq = jnp.concatenate([k_g, q_g], axis=0)
            kbd = jnp.where(bdmask_k, jnp.concatenate([k_g] * hg, axis=0), 0.0)
            res = _dot_nt(kq, kbd)
            amat = jnp.where(strict_cat, bcat * res[:chunk] * dec, 0.0)
            qkm_ref[item] = res[chunk:] * dec
            p_ref[item] = eye_cat - amat
            q_ref[item] = amat

    for item in range(nc * ng):
        amat = q_ref[item]
        q_ref[item] = _dot(amat, block_diag(amat))
    span = 2
    while span < chunk:
        last = 2 * span >= chunk
        for item in range(nc * ng):
            p = p_ref[item]
            q = q_ref[item]
            qbd = block_diag(q)
            if last:
                p_ref[item] = p + _dot(p, qbd)
            else:
                pq = _dot(jnp.concatenate([p, q], axis=0), qbd)
                p_ref[item] = p + pq[:chunk]
                q_ref[item] = pq[chunk:]
        span *= 2

    for c in range(nc):
        rows = slice(c * chunk, (c + 1) * chunk)
        gblk = g_ref[rows, :]
        bblk = beta_ref[rows, :]
        first = []
        for h in range(GD_H):
            lq = slice(E_Q + h * GD_DK, E_Q + (h + 1) * GD_DK)
            lk = slice(E_K + h * GD_DK, E_K + (h + 1) * GD_DK)
            gc = gblk[:, GD_H + h:GD_H + h + 1]
            bc = bblk[:, h:h + 1]
            eg = jnp.exp(gc)
            kh = z_ref[rows, lk]
            s_old = s_ref[h]
            r1 = _dot(jnp.concatenate([kh * (bc * eg), z_ref[rows, lq] * eg], axis=0), s_old)
            first.append((gc, bc, kh, s_old, r1))
        second = []
        for h in range(GD_H):
            gc, bc, kh, s_old, r1 = first[h]
            item = c * ng + h // hg
            li = (h % hg) * chunk
            lv = slice(E_V + h * GD_DV, E_V + (h + 1) * GD_DV)
            dlt = _dot(p_ref[item, :, li:li + chunk], z_ref[rows, lv] * bc - r1[:chunk])
            second.append(dlt)
        for h in range(GD_H):
            gc, bc, kh, s_old, r1 = first[h]
            dlt = second[h]
            item = c * ng + h // hg
            li = (h % hg) * chunk
            o = r1[chunk:] + _dot(qkm_ref[item, :, li:li + chunk], dlt)
            gl = gc[chunk - 1:chunk, :]
            s_ref[h] = jnp.exp(gl) * s_old + _dot_tn(kh * jnp.exp(gl - gc), dlt)
            on = o * lax.rsqrt(jnp.mean(o * o, axis=-1, keepdims=True) + EPS) * gng_ref[...]
            lg = slice(E_GB + h * GD_DV, E_GB + (h + 1) * GD_DV)
            ly = slice(RG_W + h * GD_DV, RG_W + (h + 1) * GD_DV)
            y_ref[rows, ly] = on * _silu(z_ref[rows, lg])

    y = jnp.dot(y_ref[...].astype(BF16), wout_ref[...], preferred_element_type=F32)
    xo_ref[...] = _layer_norm_rows(ALPHA * x + y, lng_ref[...], lnb_ref[...])

    @pl.when(t == nt - 1)
    def _fin():
        ho_ref[...] = h_ref[...]
        cvao_ref[...] = exta_ref[base:CONV_PAD, :]
        cvbo_ref[...] = extb_ref[base:CONV_PAD, :]
        so_ref[...] = s_ref[...]


def _odd_kernel(*refs, tb, chunk, layer, has_state):
    (x_ref, win_ref, mlp_ref, mng_ref, hlb_ref, hng_ref, wout_ref, lng_ref, lnb_ref) = refs[:9]
    pos = 9
    if has_state:
        c0_ref, n0_ref, m0_ref, hs0_ref = refs[pos:pos + 4]
        pos += 4
    xo_ref, co_ref, no_ref, mo_ref, hso_ref = refs[pos:pos + 5]
    pos += 5
    z_ref, y_ref, c_ref, n_ref, m_ref, st_ref, gi_ref, a1_ref, kv_ref, loc_ref, oi_ref = refs[pos:pos + 11]

    t = pl.program_id(1)
    nt = pl.num_programs(1)
    hp = HG_H // 2

    @pl.when(t == 0)
    def _init():
        if has_state:
            c_ref[...] = c0_ref[...]
            n_ref[...] = n0_ref[...]
            m_ref[...] = m0_ref[...]
            for h in range(HG_H):
                st_ref[h // 2, :, (h % 2) * HG_DK:(h % 2 + 1) * HG_DK] = hs0_ref[h].T
        else:
            c_ref[...] = jnp.zeros_like(c_ref)
            n_ref[...] = jnp.zeros_like(n_ref)
            m_ref[...] = jnp.zeros_like(m_ref)
            st_ref[...] = jnp.zeros_like(st_ref)

    x = x_ref[...]
    xb = x.astype(BF16)
    ncol = 512
    for n0 in range(0, ODD_COLS, ncol):
        n1 = min(n0 + ncol, ODD_COLS)
        z_ref[:, n0:n1] = jnp.dot(xb, win_ref[:, n0:n1], preferred_element_type=F32)

    sm = z_ref[:, O_SM:O_SM + LANES]
    ig_all = sm + mlp_ref[0:1, :]
    lf_all = _log_sigmoid(sm + mlp_ref[1:2, :])
    b_all = _chunk_cumsum(lf_all, chunk)
    lane = lax.broadcasted_iota(jnp.int32, (1, LANES), 1)
    gi_ref[...] = jnp.where(lane < ML_H, ig_all, b_all)
    sel = _selector(0)
    tril = _tril_mask(chunk)
    nc = tb // chunk

    for c in range(nc):
        rows = slice(c * chunk, (c + 1) * chunk)
        gblk = gi_ref[rows, :]
        grow = _rows_from_cols(sel, gblk)
        for h in range(ML_H):
            item = c * ML_H + h
            qh = z_ref[rows, O_MQ + h * ML_DK:O_MQ + (h + 1) * ML_DK] * (ML_DK ** -0.5)
            kh = z_ref[rows, O_MK + h * ML_DK:O_MK + (h + 1) * ML_DK]
            vh = z_ref[rows, O_MV + h * ML_DV:O_MV + (h + 1) * ML_DV].astype(BF16)
            igc = gblk[:, h:h + 1]
            bc = gblk[:, ML_H + h:ML_H + h + 1]
            dm = jnp.where(tril, bc - grow[ML_H + h:ML_H + h + 1, :] + grow[h:h + 1, :], -jnp.inf)
            md = jnp.max(dm, axis=-1, keepdims=True)
            qkl = _dot_nt(qh, kh) * jnp.exp(dm - md)
            a1_ref[item] = _dot(qkl, vh)
            kwl = kh * jnp.exp(bc[chunk - 1:chunk, :] - bc + igc - md[chunk - 1:chunk, :])
            kv_ref[item] = _dot_tn(kwl, vh)
            loc_ref[item, 0:chunk, 0:1] = md
            loc_ref[item, 0:chunk, 1:2] = jnp.sum(qkl, axis=-1, keepdims=True)
            loc_ref[item, chunk:chunk + 1, :] = jnp.sum(kwl, axis=0, keepdims=True)

    for c in range(nc):
        rows = slice(c * chunk, (c + 1) * chunk)
        gblk = gi_ref[rows, :]
        for h in range(ML_H):
            item = c * ML_H + h
            qh = z_ref[rows, O_MQ + h * ML_DK:O_MQ + (h + 1) * ML_DK] * (ML_DK ** -0.5)
            bc = gblk[:, ML_H + h:ML_H + h + 1]
            md = loc_ref[item, 0:chunk, 0:1]
            rs = loc_ref[item, 0:chunk, 1:2]
            kn = loc_ref[item, chunk:chunk + 1, :]
            m_old = m_ref[h:h + 1, 0:1]
            c_old = c_ref[h]
            n_old = n_ref[h:h + 1, :]
            inter = bc + m_old
            mt = jnp.maximum(inter, md)
            sc = jnp.exp(inter - mt)
            sd = jnp.exp(md - mt)
            num = sc * _dot(qh, c_old) + sd * a1_ref[item]
            den = sc * jnp.sum(qh * n_old, axis=-1, keepdims=True) + sd * rs
            hh = num / jnp.maximum(jnp.abs(den), jnp.exp(-mt))
            ml = mt[chunk - 1:chunk, :]
            sl = jnp.exp(inter[chunk - 1:chunk, :] - ml)
            sdl = sd[chunk - 1:chunk, :]
            c_ref[h] = sl * c_old + sdl * kv_ref[item]
            n_ref[h:h + 1, :] = sl * n_old + sdl * kn
            m_ref[h:h + 1, :] = jnp.broadcast_to(ml, (1, LANES))
            mu = jnp.mean(hh, axis=-1, keepdims=True)
            hc = hh - mu
            var = jnp.mean(hc * hc, axis=-1, keepdims=True)
            hn = hc * lax.rsqrt(var + EPS) * mng_ref[...]
            lo = slice(O_MO + h * ML_DV, O_MO + (h + 1) * ML_DV)
            lg = slice(O_MG + h * ML_DV, O_MG + (h + 1) * ML_DV)
            y_ref[rows, h * ML_DV:(h + 1) * ML_DV] = _sigmoid(z_ref[rows, lo]) * hn * _silu(z_ref[rows, lg])

    lbp = hlb_ref[...]
    e = jnp.exp(lbp - jnp.max(lbp, axis=0, keepdims=True))
    smx = e / jnp.sum(e, axis=0, keepdims=True)
    lb = smx[0:1, :] * 0.0
    for jj in range(1, layer + 1):
        lb = lb + smx[jj:jj + 1, :]
    hf = z_ref[:, O_HF:O_HF + D_MODEL]
    la = jnp.log(lb)
    lbv = jnp.log1p(-lb) + _log_sigmoid(hf)
    logf = jnp.maximum(la, lbv) + _log1p_exp_neg_abs(la - lbv)
    assert HG_BLOCK == 2 * SUBLANES
    g2 = _pair_tile_cumsum(logf) * LOG2E
    z_ref[:, O_MG:O_MG + D_MODEL] = g2
    z_ref[:, O_HF:O_HF + D_MODEL] = g2 - jnp.log((1.0 - lb) * _sigmoid(-hf)) * LOG2E
    z_ref[:, O_HQ:O_HQ + D_MODEL] = _silu(z_ref[:, O_HQ:O_HQ + D_MODEL]) * (HG_DK ** -0.5)
    half = HG_BLOCK // 2
    rowi = lax.broadcasted_iota(jnp.int32, (HG_BLOCK, 1), 0)
    kcat = HG_BLOCK * HG_DK
    sel_j = (lax.broadcasted_iota(jnp.int32, (HG_BLOCK, kcat), 0)
             == lax.shift_right_logical(lax.broadcasted_iota(jnp.int32, (HG_BLOCK, kcat), 1), 7)).astype(BF16)
    pair_mask = (lax.shift_right_logical(lax.broadcasted_iota(jnp.int32, (2 * HG_BLOCK, 2 * HG_DK), 0), 4)
                 == lax.shift_right_logical(lax.broadcasted_iota(jnp.int32, (2 * HG_BLOCK, 2 * HG_DK), 1), 7))
    zero_half = jnp.zeros((half, HG_DK), F32)
    zero_rows = jnp.zeros((LANES - HG_BLOCK, HG_H * HG_BLOCK), F32)

    ub = min(HG_UNROLL, tb // HG_BLOCK)

    def intra_group(r0):
        pcats = []
        for u in range(ub):
            rows = pl.ds(r0 + u * HG_BLOCK, HG_BLOCK)
            gb = z_ref[rows, O_MG:O_MG + D_MODEL]
            hb = z_ref[rows, O_HF:O_HF + D_MODEL]
            qb = z_ref[rows, O_HQ:O_HQ + D_MODEL]
            pieces = []
            for j in range(HG_BLOCK):
                lo = 0 if j < half else half
                pj = qb[lo:] * jnp.exp2(jnp.where(rowi[lo:] >= j, gb[lo:] - hb[j:j + 1, :], -jnp.inf))
                tiles = []
                for h in range(HG_H):
                    if lo:
                        tiles.append(zero_half)
                    tiles.append(pj[:, h * HG_DK:(h + 1) * HG_DK])
                pieces.append(jnp.concatenate(tiles, axis=0))
            pcats.append(jnp.concatenate(pieces, axis=1).astype(BF16))
        atts = []
        for u in range(ub):
            att_t = lax.dot_general(sel_j, pcats[u], (((1,), (1,)), ((), ())),
                                    preferred_element_type=F32)
            atts.append(jnp.concatenate([att_t, zero_rows], axis=0).T)
        for u in range(ub):
            rows = pl.ds(r0 + u * HG_BLOCK, HG_BLOCK)
            vb = z_ref[rows, O_HI:O_HI + D_MODEL]
            for h in range(HG_H):
                att_h = atts[u][h * HG_BLOCK:(h + 1) * HG_BLOCK, 0:HG_BLOCK]
                oi_ref[rows, h * HG_DV:(h + 1) * HG_DV] = _dot(att_h, vb[:, h * HG_DV:(h + 1) * HG_DV])

    def state_group(r0):
        for u in range(ub):
            rows = pl.ds(r0 + u * HG_BLOCK, HG_BLOCK)
            gb = z_ref[rows, O_MG:O_MG + D_MODEL]
            gl = gb[HG_BLOCK - 1:HG_BLOCK, :]
            egl = jnp.exp2(gl)
            qe = z_ref[rows, O_HQ:O_HQ + D_MODEL] * jnp.exp2(gb)
            ke = jnp.exp2(gl - z_ref[rows, O_HF:O_HF + D_MODEL])
            vb = z_ref[rows, O_HI:O_HI + D_MODEL]
            for h in range(HG_H):
                ly = slice(D_MODEL + h * HG_DV, D_MODEL + (h + 1) * HG_DV)
                st_old = st_ref[h // 2, :, (h % 2) * HG_DK:(h % 2 + 1) * HG_DK]
                o = oi_ref[rows, h * HG_DV:(h + 1) * HG_DV] + _dot_nt(qe[:, h * HG_DK:(h + 1) * HG_DK], st_old)
                on = o * lax.rsqrt(jnp.mean(o * o, axis=-1, keepdims=True) + EPS) * hng_ref[...]
                lgt = slice(O_HGATE + h * HG_DV, O_HGATE + (h + 1) * HG_DV)
                y_ref[rows, ly] = on * _silu(z_ref[rows, lgt])
            for p in range(hp):
                lp = slice(2 * p * HG_DK, (2 * p + 2) * HG_DK)
                vst = jnp.concatenate([vb[:, 2 * p * HG_DV:(2 * p + 1) * HG_DV],
                                       vb[:, (2 * p + 1) * HG_DV:(2 * p + 2) * HG_DV]], axis=0)
                kep = ke[:, lp]
                rhs = jnp.where(pair_mask, jnp.concatenate([kep, kep], axis=0), 0.0)
                st_ref[p] = egl[:, lp] * st_ref[p] + _dot_tn(vst, rhs)

    grp = ub * HG_BLOCK
    ngrp = tb // grp
    intra_group(0)

    def pipe_body(gi, carry):
        r0 = pl.multiple_of(gi * grp, grp)
        state_group(r0)
        intra_group(r0 + grp)
        return carry

    lax.fori_loop(0, ngrp - 1, pipe_body, 0)
    state_group((ngrp - 1) * grp)

    y = jnp.dot(y_ref[...].astype(BF16), wout_ref[...], preferred_element_type=F32)
    xo_ref[...] = _layer_norm_rows(ALPHA * x + y, lng_ref[...], lnb_ref[...])

    @pl.when(t == nt - 1)
    def _fin():
        co_ref[...] = c_ref[...]
        no_ref[...] = n_ref[...]
        mo_ref[...] = m_ref[...]
        for h in range(HG_H):
            hso_ref[h] = st_ref[h // 2, :, (h % 2) * HG_DK:(h % 2 + 1) * HG_DK].T


def _const_spec(shape):
    nd = len(shape)
    return pl.BlockSpec(shape, lambda b, t: (0,) * nd, pipeline_mode=pl.Buffered(1))


def _batch_spec(shape):
    nd = len(shape)
    return pl.BlockSpec((None,) + tuple(shape), lambda b, t: (b,) + (0,) * nd)


def _time_block(T, tb_max):
    tb = min(T, tb_max)
    while T % tb:
        tb -= 1
    return tb


def _chunk_len(tb, max_blk):
    c = min(max_blk, tb)
    while tb % c:
        c -= 1
    return c


def _even_layer(x, params, state, tb_max):
    B, T, D = x.shape
    tb = _time_block(T, tb_max)
    chunk = _chunk_len(tb, CHUNK)
    has_state = state is not None
    (win, cwa, cba, wai, bai, lam, cwb, gdp, gng, wout, lng, lnb) = params
    consts = [win, cwa, cba, wai, bai, lam, cwb, gdp, gng, wout, lng, lnb]
    in_specs = [pl.BlockSpec((None, tb, D), lambda b, t: (b, t, 0))]
    in_specs += [_const_spec(c.shape) for c in consts]
    args = [x] + consts
    state_shapes = [(1, RG_W), (CONV_W - 1, RG_W), (GD_H, GD_DK, GD_DV), (CONV_W - 1, GD_CONV)]
    if has_state:
        h0, cva0, s0, cvb0 = state
        args += [h0.reshape(B, 1, RG_W), cva0, s0, cvb0]
        in_specs += [_batch_spec(s) for s in state_shapes]
    out_shape = [jax.ShapeDtypeStruct((B, T, D), x.dtype)]
    out_shape += [jax.ShapeDtypeStruct((B,) + s, x.dtype) for s in state_shapes]
    out_specs = [pl.BlockSpec((None, tb, D), lambda b, t: (b, t, 0))]
    out_specs += [_batch_spec(s) for s in state_shapes]
    scratch = [
        pltpu.VMEM((tb, EVEN_COLS), F32),
        pltpu.VMEM((tb, 2 * D_MODEL), F32),
        pltpu.VMEM((CONV_PAD + tb, RG_W), F32),
        pltpu.VMEM((CONV_PAD + tb, GD_CONV), F32),
        pltpu.VMEM((1, RG_W), F32),
        pltpu.VMEM((GD_H, GD_DK, GD_DV), F32),
        pltpu.VMEM((tb, LANES), F32),
        pltpu.VMEM((tb, LANES), F32),
    ]
    n_items = (tb // chunk) * (GD_H * chunk // (2 * LANES))
    scratch += [pltpu.VMEM((n_items, chunk, 2 * LANES), F32) for _ in range(3)]
    outs = pl.pallas_call(
        functools.partial(_even_kernel, tb=tb, chunk=chunk, has_state=has_state),
        grid=(B, T // tb),
        in_specs=in_specs,
        out_specs=out_specs,
        out_shape=out_shape,
        scratch_shapes=scratch,
        compiler_params=pltpu.CompilerParams(
            dimension_semantics=("arbitrary", "arbitrary"), vmem_limit_bytes=VMEM_LIMIT_BYTES),
        name="even_layer_state" if has_state else "even_layer",
    )(*args)
    xo, ho, cvao, so, cvbo = outs
    return xo, ho.reshape(B, RG_W), cvao, so, cvbo


def _odd_layer(x, params, layer, state, tb_max):
    B, T, D = x.shape
    tb = _time_block(T, tb_max)
    chunk = _chunk_len(tb, CHUNK)
    has_state = state is not None
    (win, mlp, mng, hlb, hng, wout, lng, lnb) = params
    consts = [win, mlp, mng, hlb, hng, wout, lng, lnb]
    in_specs = [pl.BlockSpec((None, tb, D), lambda b, t: (b, t, 0))]
    in_specs += [_const_spec(c.shape) for c in consts]
    args = [x] + consts
    state_shapes = [(ML_H, ML_DK, ML_DV), (SUBLANES, ML_DK), (SUBLANES, LANES), (HG_H, HG_DK, HG_DV)]
    if has_state:
        c0, n0, m0, hs0 = state
        n0p = jnp.zeros((B, SUBLANES, ML_DK), F32).at[:, :ML_H, :].set(n0)
        m0p = jnp.zeros((B, SUBLANES, LANES), F32).at[:, :ML_H, :].set(
            jnp.broadcast_to(m0[:, :, None], (B, ML_H, LANES)))
        args += [c0, n0p, m0p, hs0]
        in_specs += [_batch_spec(s) for s in state_shapes]
    out_shape = [jax.ShapeDtypeStruct((B, T, D), x.dtype)]
    out_shape += [jax.ShapeDtypeStruct((B,) + s, x.dtype) for s in state_shapes]
    out_specs = [pl.BlockSpec((None, tb, D), lambda b, t: (b, t, 0))]
    out_specs += [_batch_spec(s) for s in state_shapes]
    scratch = [
        pltpu.VMEM((tb, ODD_COLS), F32),
        pltpu.VMEM((tb, 2 * D_MODEL), F32),
        pltpu.VMEM((ML_H, ML_DK, ML_DV), F32),
        pltpu.VMEM((SUBLANES, ML_DK), F32),
        pltpu.VMEM((SUBLANES, LANES), F32),
        pltpu.VMEM((HG_H // 2, HG_DV, 2 * HG_DK), F32),
        pltpu.VMEM((tb, LANES), F32),
        pltpu.VMEM(((tb // chunk) * ML_H, chunk, ML_DV), F32),
        pltpu.VMEM(((tb // chunk) * ML_H, ML_DK, ML_DV), F32),
        pltpu.VMEM(((tb // chunk) * ML_H, chunk + SUBLANES, LANES), F32),
        pltpu.VMEM((tb, D_MODEL), F32),
    ]
    outs = pl.pallas_call(
        functools.partial(_odd_kernel, tb=tb, chunk=chunk, layer=layer, has_state=has_state),
        grid=(B, T // tb),
        in_specs=in_specs,
        out_specs=out_specs,
        out_shape=out_shape,
        scratch_shapes=scratch,
        compiler_params=pltpu.CompilerParams(
            dimension_semantics=("arbitrary", "arbitrary"), vmem_limit_bytes=VMEM_LIMIT_BYTES),
        name="odd_layer_state" if has_state else "odd_layer",
    )(*args)
    xo, co, no, mo, hso = outs
    return xo, co, no[:, :ML_H, :], mo[:, :ML_H, 0], hso


def _pad_lanes(cols):
    return jnp.pad(cols, ((0, 0), (0, LANES - cols.shape[1])))


def _even_params(j, w_in_even, rg_conv_w, rg_conv_b, rg_w_a, rg_b_a, rg_w_i, rg_b_i, rg_lambda,
                 gdn_conv_w, gdn_a_log, gdn_dt_bias, gdn_norm_g, w_out_even, ln_even_g, ln_even_b):
    w = w_in_even[j]
    xa, ga, qb, kb, vb = (w[:, i * 1024:(i + 1) * 1024] for i in range(5))
    small = w[:, 5120:5136]
    gb = w[:, 5136:6160]
    win = jnp.concatenate([xa, gb, ga, qb, kb, vb, _pad_lanes(small)], axis=1).astype(BF16)
    wai = jnp.concatenate([rg_w_a[j], rg_w_i[j]], axis=-1).astype(BF16)
    bai = jnp.stack([rg_b_a[j], rg_b_i[j]]).astype(F32)
    gdp = jnp.zeros((2, LANES), F32)
    gdp = gdp.at[0, GD_H:2 * GD_H].set(gdn_a_log[j]).at[1, GD_H:2 * GD_H].set(gdn_dt_bias[j])
    return (win, rg_conv_w[j], rg_conv_b[j][None, :], wai, bai, rg_lambda[j][None, :],
            gdn_conv_w[j], gdp, gdn_norm_g[j][None, :], w_out_even[j].astype(BF16),
            ln_even_g[j][None, :], ln_even_b[j][None, :])


def _odd_params(j, w_in_odd, ml_b_i, ml_b_f, ml_norm_g, hg_lb, hg_norm_g, w_out_odd, ln_odd_g, ln_odd_b):
    w = w_in_odd[j]
    mq, mk, mv = w[:, 0:512], w[:, 512:1024], w[:, 1024:2048]
    small = w[:, 2048:2056]
    mo, mg, hq, hf, hi, hgate = (w[:, 2056 + i * 1024:2056 + (i + 1) * 1024] for i in range(6))
    win = jnp.concatenate([mo, hgate, mg, mq, mk, mv, hq, hf, hi, _pad_lanes(small)], axis=1).astype(BF16)
    mlp = jnp.zeros((2, LANES), F32)
    mlp = mlp.at[0, 0:ML_H].set(ml_b_i[j]).at[1, ML_H:2 * ML_H].set(ml_b_f[j])
    return (win, mlp, ml_norm_g[j][None, :], hg_lb.astype(F32), hg_norm_g[j][None, :],
            w_out_odd[j].astype(BF16), ln_odd_g[j][None, :], ln_odd_b[j][None, :])


def _run(x, states, even_p, odd_p, tb_max):
    rg_h, rg_cv, gd_s, gd_cv, ml_c, ml_n, ml_m, hg_s = ([] for _ in range(8))
    for l in range(DEPTH):
        j = l // 2
        if l % 2 == 0:
            st = None if states is None else (states[0][j], states[1][j], states[2][j], states[3][j])
            x, h, cva, s, cvb = _even_layer(x, even_p[j], st, tb_max)
            rg_h.append(h); rg_cv.append(cva); gd_s.append(s); gd_cv.append(cvb)
        else:
            st = None if states is None else (states[4][j], states[5][j], states[6][j], states[7][j])
            x, c, n, m, s = _odd_layer(x, odd_p[j], j, st, tb_max)
            ml_c.append(c); ml_n.append(n); ml_m.append(m); hg_s.append(s)
    return (x, jnp.stack(rg_h), jnp.stack(rg_cv), jnp.stack(gd_s), jnp.stack(gd_cv),
            jnp.stack(ml_c), jnp.stack(ml_n), jnp.stack(ml_m), jnp.stack(hg_s))


def _forward(x_prompt, x_sample, states, even_w, odd_w, tb_max=TIME_BLOCK):
    n_even = even_w[0].shape[0]
    n_odd = odd_w[0].shape[0]
    even_p = [_even_params(j, *even_w) for j in range(n_even)]
    odd_p = [_odd_params(j, *odd_w) for j in range(n_odd)]
    p = _run(x_prompt, None, even_p, odd_p, tb_max)
    s = _run(x_sample, states, even_p, odd_p, tb_max)
    return (p[0], s[0]) + p[1:] + s[1:]


def kernel(x_prompt, x_sample, state_rglru_h, state_rglru_conv, state_gdn_S, state_gdn_conv, state_mlstm_C, state_mlstm_n, state_mlstm_m, state_hgrn_S, w_in_even, rg_conv_w, rg_conv_b, rg_w_a, rg_b_a, rg_w_i, rg_b_i, rg_lambda, gdn_conv_w, gdn_a_log, gdn_dt_bias, gdn_norm_g, w_out_even, ln_even_g, ln_even_b, w_in_odd, ml_b_i, ml_b_f, ml_norm_g, hg_lb, hg_norm_g, w_out_odd, ln_odd_g, ln_odd_b):
    states = (state_rglru_h, state_rglru_conv, state_gdn_S, state_gdn_conv,
              state_mlstm_C, state_mlstm_n, state_mlstm_m, state_hgrn_S)
    even_w = (w_in_even, rg_conv_w, rg_conv_b, rg_w_a, rg_b_a, rg_w_i, rg_b_i, rg_lambda,
              gdn_conv_w, gdn_a_log, gdn_dt_bias, gdn_norm_g, w_out_even, ln_even_g, ln_even_b)
    odd_w = (w_in_odd, ml_b_i, ml_b_f, ml_norm_g, hg_lb, hg_norm_g, w_out_odd, ln_odd_g, ln_odd_b)
    return _forward(x_prompt, x_sample, states, even_w, odd_w)
```

```python
import functools

import jax
import jax.numpy as jnp
from jax import lax
from jax.experimental import pallas as pl
from jax.experimental.pallas import tpu as pltpu

F32 = jnp.float32
BF16 = jnp.bfloat16

D_MODEL = 1024
DEPTH = 4
CONV_W = 4
ALPHA = (2 * DEPTH) ** 0.25
EPS = 1e-5
RG_W = D_MODEL
RG_BLOCKS = 8
RG_BW = RG_W // RG_BLOCKS
RG_C = 8.0
GD_H = 8
GD_DK = D_MODEL // GD_H
GD_DV = D_MODEL // GD_H
GD_CONV = 3 * D_MODEL
ML_H = 4
ML_DK = D_MODEL // (2 * ML_H)
ML_DV = D_MODEL // ML_H
ML_QK = ML_H * ML_DK
HG_H = 8
HG_DK = D_MODEL // HG_H
HG_DV = D_MODEL // HG_H
HG_BLOCK = 16
CHUNK = 64
HG_UNROLL = 4
LOG2E = 1.4426950408889634

LANES = 128
SUBLANES = 8
CONV_PAD = SUBLANES
TIME_BLOCK = 256
VMEM_LIMIT_BYTES = 56 * 1024 * 1024

E_XA, E_GB, E_GA, E_Q, E_K, E_V, E_SM = 0, 1024, 2048, 3072, 4096, 5120, 6144
EVEN_COLS = E_SM + LANES
O_MO, O_HGATE, O_MG, O_MQ, O_MK, O_MV, O_HQ, O_HF, O_HI, O_SM = (
    0, 1024, 2048, 3072, 3584, 4096, 5120, 6144, 7168, 8192)
ODD_COLS = O_SM + LANES


def _dot(a, b):
    return jnp.dot(a.astype(BF16), b.astype(BF16), preferred_element_type=F32)


def _dot_nt(a, b):
    return lax.dot_general(a.astype(BF16), b.astype(BF16), (((1,), (1,)), ((), ())),
                           preferred_element_type=F32)


def _dot_tn(a, b):
    return lax.dot_general(a.astype(BF16), b.astype(BF16), (((0,), (0,)), ((), ())),
                           preferred_element_type=F32)


def _rows_from_cols(sel, m):
    hi = m.astype(BF16)
    lo = (m - hi.astype(F32)).astype(BF16)
    dn = (((1,), (1,)), ((), ()))
    out = lax.dot_general(sel, hi, dn, preferred_element_type=F32)
    out += lax.dot_general(sel, lo, dn, preferred_element_type=F32)
    return out


def _selector(lane0):
    r = lax.broadcasted_iota(jnp.int32, (SUBLANES, LANES), 0)
    c = lax.broadcasted_iota(jnp.int32, (SUBLANES, LANES), 1)
    return (c == r + lane0).astype(BF16)


def _sigmoid(x):
    return jax.nn.sigmoid(x)


def _silu(x):
    return x * jax.nn.sigmoid(x)


def _log1p_exp_neg_abs(x):
    return jnp.log(1.0 + jnp.exp(-jnp.abs(x)))


def _softplus(x):
    return jnp.maximum(x, 0.0) + _log1p_exp_neg_abs(x)


def _log_sigmoid(x):
    return jnp.minimum(x, 0.0) - _log1p_exp_neg_abs(x)


def _chunk_cumsum(x, chunk):
    n = x.shape[0]
    pos = lax.broadcasted_iota(jnp.int32, (n, 1), 0) % chunk
    d = 1
    while d < chunk:
        x = x + jnp.where(pos >= d, pltpu.roll(x, d, 0), 0.0)
        d *= 2
    return x


def _pair_tile_cumsum(x):
    n, w = x.shape
    x3 = x.reshape(n // SUBLANES, SUBLANES, w)
    sub = lax.broadcasted_iota(jnp.int32, (1, SUBLANES, 1), 1)
    d = 1
    while d < SUBLANES:
        x3 = x3 + jnp.where(sub >= d, pltpu.roll(x3, d, 1), 0.0)
        d *= 2
    x4 = x3.reshape(n // (2 * SUBLANES), 2, SUBLANES, w)
    first = x4[:, 0]
    second = x4[:, 1] + first[:, SUBLANES - 1:SUBLANES, :]
    return jnp.concatenate([first[:, None], second[:, None]], axis=1).reshape(n, w)


def _causal_conv(ext_ref, u, w_ref, tb):
    ext_ref[CONV_PAD:CONV_PAD + tb, :] = u
    base = CONV_PAD - (CONV_W - 1)
    nt8 = tb // SUBLANES
    wdt = u.shape[1]
    ext3 = ext_ref[...].reshape(nt8 + 1, SUBLANES, wdt)
    sub = lax.broadcasted_iota(jnp.int32, (1, SUBLANES, 1), 1)
    y = u.reshape(nt8, SUBLANES, wdt) * w_ref[CONV_W - 1:CONV_W, :]
    for s in range(1, CONV_W):
        r3 = pltpu.roll(ext3, s, 1)
        y = y + jnp.where(sub < s, r3[:nt8], r3[1:]) * w_ref[CONV_W - 1 - s:CONV_W - s, :]
    ext_ref[base:CONV_PAD, :] = ext_ref[base + tb:CONV_PAD + tb, :]
    return y.reshape(tb, wdt)


def _layer_norm_rows(x, g, b):
    mu = jnp.mean(x, axis=-1, keepdims=True)
    xc = x - mu
    var = jnp.mean(xc * xc, axis=-1, keepdims=True)
    return xc * lax.rsqrt(var + EPS) * g + b


def _tril_mask(c):
    r = lax.broadcasted_iota(jnp.int32, (c, c), 0)
    s = lax.broadcasted_iota(jnp.int32, (c, c), 1)
    return s <= r


def _even_kernel(*refs, tb, chunk, has_state):
    (x_ref, win_ref, cwa_ref, cba_ref, wai_ref, bai_ref, lam_ref, cwb_ref, gdp_ref, gng_ref,
     wout_ref, lng_ref, lnb_ref) = refs[:13]
    pos = 13
    if has_state:
        h0_ref, cva0_ref, s0_ref, cvb0_ref = refs[pos:pos + 4]
        pos += 4
    xo_ref, ho_ref, cvao_ref, so_ref, cvbo_ref = refs[pos:pos + 5]
    pos += 5
    (z_ref, y_ref, exta_ref, extb_ref, h_ref, s_ref, g_ref, beta_ref,
     p_ref, q_ref, qkm_ref) = refs[pos:pos + 11]

    t = pl.program_id(1)
    nt = pl.num_programs(1)
    base = CONV_PAD - (CONV_W - 1)

    @pl.when(t == 0)
    def _init():
        exta_ref[0:CONV_PAD, :] = jnp.zeros((CONV_PAD, RG_W), F32)
        extb_ref[0:CONV_PAD, :] = jnp.zeros((CONV_PAD, GD_CONV), F32)
        if has_state:
            h_ref[...] = h0_ref[...]
            exta_ref[base:CONV_PAD, :] = cva0_ref[...]
            extb_ref[base:CONV_PAD, :] = cvb0_ref[...]
            s_ref[...] = s0_ref[...]
        else:
            h_ref[...] = jnp.zeros_like(h_ref)
            s_ref[...] = jnp.zeros_like(s_ref)

    x = x_ref[...]
    xb = x.astype(BF16)
    ncol = 512
    for n0 in range(0, EVEN_COLS, ncol):
        n1 = min(n0 + ncol, EVEN_COLS)
        z_ref[:, n0:n1] = jnp.dot(xb, win_ref[:, n0:n1], preferred_element_type=F32)

    xa = _causal_conv(exta_ref, z_ref[:, E_XA:E_XA + RG_W], cwa_ref, tb) + cba_ref[...]
    gates = []
    for g in range(RG_BLOCKS):
        gates.append(_dot(xa[:, g * RG_BW:(g + 1) * RG_BW], wai_ref[g]))
    pre_r = jnp.concatenate([gt[:, :RG_BW] for gt in gates], axis=-1) + bai_ref[0:1, :]
    pre_i = jnp.concatenate([gt[:, RG_BW:] for gt in gates], axis=-1) + bai_ref[1:2, :]
    r = _sigmoid(pre_r)
    i = _sigmoid(pre_i)
    log_a = (-RG_C) * r * _softplus(-lam_ref[...])
    a = jnp.exp(log_a)
    th = jnp.tanh(log_a)
    u = jnp.sqrt((-2.0) * th / (1.0 - th)) * (i * xa)
    nt8 = tb // SUBLANES
    a3 = a.reshape(nt8, SUBLANES, RG_W)
    u3 = u.reshape(nt8, SUBLANES, RG_W)
    sub = lax.broadcasted_iota(jnp.int32, (1, SUBLANES, 1), 1)
    d = 1
    while d < SUBLANES:
        m = sub >= d
        u3 = a3 * jnp.where(m, pltpu.roll(u3, d, 1), 0.0) + u3
        a3 = a3 * jnp.where(m, pltpu.roll(a3, d, 1), 1.0)
        d *= 2
    hprev = h_ref[...]
    for r in range(nt8):
        ht = u3[r] + a3[r] * hprev
        hprev = ht[SUBLANES - 1:SUBLANES, :]
        rows8 = slice(r * SUBLANES, (r + 1) * SUBLANES)
        y_ref[rows8, 0:RG_W] = ht * _silu(z_ref[rows8, E_GA:E_GA + RG_W])
    h_ref[...] = hprev

    qkv = _silu(_causal_conv(extb_ref, z_ref[:, E_Q:E_Q + GD_CONV], cwb_ref, tb))
    z_ref[:, E_Q:E_Q + GD_CONV] = qkv
    sm = z_ref[:, E_SM:E_SM + LANES]
    beta_ref[...] = _sigmoid(sm)
    g_all = -jnp.exp(gdp_ref[0:1, :]) * _softplus(sm + gdp_ref[1:2, :])
    g_ref[...] = _chunk_cumsum(g_all, chunk)
    for h in range(GD_H):
        lq = slice(E_Q + h * GD_DK, E_Q + (h + 1) * GD_DK)
        lk = slice(E_K + h * GD_DK, E_K + (h + 1) * GD_DK)
        qh = z_ref[:, lq]
        z_ref[:, lq] = qh * (lax.rsqrt(jnp.sum(qh * qh, axis=-1, keepdims=True) + 1e-6) * (GD_DK ** -0.5))
        kh = z_ref[:, lk]
        z_ref[:, lk] = kh * lax.rsqrt(jnp.sum(kh * kh, axis=-1, keepdims=True) + 1e-6)

    gw = 2 * LANES
    hg = gw // chunk
    ng = GD_H // hg
    nc = tb // chunk
    lc = chunk.bit_length() - 1
    assert 1 << lc == chunk and hg * chunk == gw and ng * hg == GD_H

    def _iota(shape, dim):
        return lax.broadcasted_iota(jnp.int32, shape, dim)

    ri = _iota((chunk, gw), 0)
    cj = jnp.bitwise_and(_iota((chunk, gw), 1), chunk - 1)
    tril_cat = cj <= ri
    strict_cat = cj < ri
    eye_cat = (cj == ri).astype(F32)
    bdmask = (lax.shift_right_logical(_iota((gw, gw), 0), lc)
              == lax.shift_right_logical(_iota((gw, gw), 1), lc))
    kw = hg * GD_DK
    bdmask_k = (lax.shift_right_logical(_iota((gw, kw), 0), lc)
                == lax.shift_right_logical(_iota((gw, kw), 1), 7))
    ex = (_iota((LANES, 2 * GD_H * chunk), 0)
          == lax.shift_right_logical(_iota((LANES, 2 * GD_H * chunk), 1), lc)).astype(BF16)
    lane = _iota((1, LANES), 1)
    sel = _selector(GD_H)

    def block_diag(xcat):
        return jnp.where(bdmask, jnp.concatenate([xcat] * hg, axis=0), 0.0).astype(BF16)

    for c in range(nc):
        rows = slice(c * chunk, (c + 1) * chunk)
        gblk = g_ref[rows, :]
        bblk = beta_ref[rows, :]
        grow = _rows_from_cols(sel, gblk)
        gr_all = jnp.concatenate([grow[h:h + 1, :] for h in range(GD_H)], axis=-1)
        m = jnp.where(lane < GD_H, bblk, gblk)
        hi = m.astype(BF16)
        lo = (m - hi.astype(F32)).astype(BF16)
        ex3 = jnp.dot(hi, ex, preferred_element_type=F32) + jnp.dot(lo, ex, preferred_element_type=F32)
        for g in range(ng):
            item = c * ng + g
            bcat = ex3[:, g * gw:(g + 1) * gw]
            gcat = ex3[:, GD_H * chunk + g * gw:GD_H * chunk + (g + 1) * gw]
            dec = jnp.exp(jnp.where(tril_cat, gcat - gr_all[:, g * gw:(g + 1) * gw], -jnp.inf))
            k_g = z_ref[rows, E_K + g * kw:E_K + (g + 1) * kw]
            q_g = z_ref[rows, E_Q + g * kw:E_Q + (g + 1) * kw]
            kq = jnp.concatenate([k_g, q_g], axis=0)
            kbd = jnp.where(bdmask_k, jnp.concatenate([k_g] * hg, axis=0), 0.0)
            res = _dot_nt(kq, kbd)
            amat = jnp.where(strict_cat, bcat * res[:chunk] * dec, 0.0)
            qkm_ref[item] = res[chunk:] * dec
            p_ref[item] = eye_cat - amat
            q_ref[item] = amat

    for item in range(nc * ng):
        amat = q_ref[item]
        q_ref[item] = _dot(amat, block_diag(amat))
    span = 2
    while span < chunk:
        last = 2 * span >= chunk
        for item in range(nc * ng):
            p = p_ref[item]
            q = q_ref[item]
            qbd = block_diag(q)
            if last:
                p_ref[item] = p + _dot(p, qbd)
            else:
                pq = _dot(jnp.concatenate([p, q], axis=0), qbd)
                p_ref[item] = p + pq[:chunk]
                q_ref[item] = pq[chunk:]
        span *= 2

    def recur_body(c, carry):
        rows = pl.ds(pl.multiple_of(c * chunk, chunk), chunk)
        gblk = g_ref[rows, :]
        bblk = beta_ref[rows, :]
        first = []
        for h in range(GD_H):
            lq = slice(E_Q + h * GD_DK, E_Q + (h + 1) * GD_DK)
            lk = slice(E_K + h * GD_DK, E_K + (h + 1) * GD_DK)
            gc = gblk[:, GD_H + h:GD_H + h + 1]
            bc = bblk[:, h:h + 1]
            eg = jnp.exp(gc)
            kh = z_ref[rows, lk]
            s_old = s_ref[h]
            r1 = _dot(jnp.concatenate([kh * (bc * eg), z_ref[rows, lq] * eg], axis=0), s_old)
            first.append((gc, bc, kh, s_old, r1))
        second = []
        for h in range(GD_H):
            gc, bc, kh, s_old, r1 = first[h]
            item = c * ng + h // hg
            li = (h % hg) * chunk
            lv = slice(E_V + h * GD_DV, E_V + (h + 1) * GD_DV)
            dlt = _dot(p_ref[item, :, li:li + chunk], z_ref[rows, lv] * bc - r1[:chunk])
            second.append(dlt)
        for h in range(GD_H):
            gc, bc, kh, s_old, r1 = first[h]
            dlt = second[h]
            item = c * ng + h // hg
            li = (h % hg) * chunk
            o = r1[chunk:] + _dot(qkm_ref[item, :, li:li + chunk], dlt)
            gl = gc[chunk - 1:chunk, :]
            s_ref[h] = jnp.exp(gl) * s_old + _dot_tn(kh * jnp.exp(gl - gc), dlt)
            on = o * lax.rsqrt(jnp.mean(o * o, axis=-1, keepdims=True) + EPS) * gng_ref[...]
            lg = slice(E_GB + h * GD_DV, E_GB + (h + 1) * GD_DV)
            ly = slice(RG_W + h * GD_DV, RG_W + (h + 1) * GD_DV)
            y_ref[rows, ly] = on * _silu(z_ref[rows, lg])
        return carry

    lax.fori_loop(0, nc, recur_body, 0)

    y = jnp.dot(y_ref[...].astype(BF16), wout_ref[...], preferred_element_type=F32)
    xo_ref[...] = _layer_norm_rows(ALPHA * x + y, lng_ref[...], lnb_ref[...])

    @pl.when(t == nt - 1)
    def _fin():
        ho_ref[...] = h_ref[...]
        cvao_ref[...] = exta_ref[base:CONV_PAD, :]
        cvbo_ref[...] = extb_ref[base:CONV_PAD, :]
        so_ref[...] = s_ref[...]


def _odd_kernel(*refs, tb, chunk, layer, has_state):
    (x_ref, win_ref, mlp_ref, mng_ref, hlb_ref, hng_ref, wout_ref, lng_ref, lnb_ref) = refs[:9]
    pos = 9
    if has_state:
        c0_ref, n0_ref, m0_ref, hs0_ref = refs[pos:pos + 4]
        pos += 4
    xo_ref, co_ref, no_ref, mo_ref, hso_ref = refs[pos:pos + 5]
    pos += 5
    z_ref, y_ref, c_ref, n_ref, m_ref, st_ref, gi_ref, a1_ref, kv_ref, loc_ref, oi_ref = refs[pos:pos + 11]

    t = pl.program_id(1)
    nt = pl.num_programs(1)
    hp = HG_H // 2

    @pl.when(t == 0)
    def _init():
        if has_state:
            c_ref[...] = c0_ref[...]
            n_ref[...] = n0_ref[...]
            m_ref[...] = m0_ref[...]
            for h in range(HG_H):
                st_ref[h // 2, :, (h % 2) * HG_DK:(h % 2 + 1) * HG_DK] = hs0_ref[h].T
        else:
            c_ref[...] = jnp.zeros_like(c_ref)
            n_ref[...] = jnp.zeros_like(n_ref)
            m_ref[...] = jnp.zeros_like(m_ref)
            st_ref[...] = jnp.zeros_like(st_ref)

    x = x_ref[...]
    xb = x.astype(BF16)
    ncol = 512
    for n0 in range(0, ODD_COLS, ncol):
        n1 = min(n0 + ncol, ODD_COLS)
        z_ref[:, n0:n1] = jnp.dot(xb, win_ref[:, n0:n1], preferred_element_type=F32)

    sm = z_ref[:, O_SM:O_SM + LANES]
    ig_all = sm + mlp_ref[0:1, :]
    lf_all = _log_sigmoid(sm + mlp_ref[1:2, :])
    b_all = _chunk_cumsum(lf_all, chunk)
    lane = lax.broadcasted_iota(jnp.int32, (1, LANES), 1)
    gi_ref[...] = jnp.where(lane < ML_H, ig_all, b_all)
    sel = _selector(0)
    tril = _tril_mask(chunk)
    nc = tb // chunk

    for c in range(nc):
        rows = slice(c * chunk, (c + 1) * chunk)
        gblk = gi_ref[rows, :]
        grow = _rows_from_cols(sel, gblk)
        for h in range(ML_H):
            item = c * ML_H + h
            qh = z_ref[rows, O_MQ + h * ML_DK:O_MQ + (h + 1) * ML_DK] * (ML_DK ** -0.5)
            kh = z_ref[rows, O_MK + h * ML_DK:O_MK + (h + 1) * ML_DK]
            vh = z_ref[rows, O_MV + h * ML_DV:O_MV + (h + 1) * ML_DV].astype(BF16)
            igc = gblk[:, h:h + 1]
            bc = gblk[:, ML_H + h:ML_H + h + 1]
            dm = jnp.where(tril, bc - grow[ML_H + h:ML_H + h + 1, :] + grow[h:h + 1, :], -jnp.inf)
            md = jnp.max(dm, axis=-1, keepdims=True)
            qkl = _dot_nt(qh, kh) * jnp.exp(dm - md)
            a1_ref[item] = _dot(qkl, vh)
            kwl = kh * jnp.exp(bc[chunk - 1:chunk, :] - bc + igc - md[chunk - 1:chunk, :])
            kv_ref[item] = _dot_tn(kwl, vh)
            loc_ref[item, 0:chunk, 0:1] = md
            loc_ref[item, 0:chunk, 1:2] = jnp.sum(qkl, axis=-1, keepdims=True)
            loc_ref[item, chunk:chunk + 1, :] = jnp.sum(kwl, axis=0, keepdims=True)

    for c in range(nc):
        rows = slice(c * chunk, (c + 1) * chunk)
        gblk = gi_ref[rows, :]
        for h in range(ML_H):
            item = c * ML_H + h
            qh = z_ref[rows, O_MQ + h * ML_DK:O_MQ + (h + 1) * ML_DK] * (ML_DK ** -0.5)
            bc = gblk[:, ML_H + h:ML_H + h + 1]
            md = loc_ref[item, 0:chunk, 0:1]
            rs = loc_ref[item, 0:chunk, 1:2]
            kn = loc_ref[item, chunk:chunk + 1, :]
            m_old = m_ref[h:h + 1, 0:1]
            c_old = c_ref[h]
            n_old = n_ref[h:h + 1, :]
            inter = bc + m_old
            mt = jnp.maximum(inter, md)
            sc = jnp.exp(inter - mt)
            sd = jnp.exp(md - mt)
            num = sc * _dot(qh, c_old) + sd * a1_ref[item]
            den = sc * jnp.sum(qh * n_old, axis=-1, keepdims=True) + sd * rs
            hh = num / jnp.maximum(jnp.abs(den), jnp.exp(-mt))
            ml = mt[chunk - 1:chunk, :]
            sl = jnp.exp(inter[chunk - 1:chunk, :] - ml)
            sdl = sd[chunk - 1:chunk, :]
            c_ref[h] = sl * c_old + sdl * kv_ref[item]
            n_ref[h:h + 1, :] = sl * n_old + sdl * kn
            m_ref[h:h + 1, :] = jnp.broadcast_to(ml, (1, LANES))
            mu = jnp.mean(hh, axis=-1, keepdims=True)
            hc = hh - mu
            var = jnp.mean(hc * hc, axis=-1, keepdims=True)
            hn = hc * lax.rsqrt(var + EPS) * mng_ref[...]
            lo = slice(O_MO + h * ML_DV, O_MO + (h + 1) * ML_DV)
            lg = slice(O_MG + h * ML_DV, O_MG + (h + 1) * ML_DV)
            y_ref[rows, h * ML_DV:(h + 1) * ML_DV] = _sigmoid(z_ref[rows, lo]) * hn * _silu(z_ref[rows, lg])

    lbp = hlb_ref[...]
    e = jnp.exp(lbp - jnp.max(lbp, axis=0, keepdims=True))
    smx = e / jnp.sum(e, axis=0, keepdims=True)
    lb = smx[0:1, :] * 0.0
    for jj in range(1, layer + 1):
        lb = lb + smx[jj:jj + 1, :]
    hf = z_ref[:, O_HF:O_HF + D_MODEL]
    la = jnp.log(lb)
    lbv = jnp.log1p(-lb) + _log_sigmoid(hf)
    logf = jnp.maximum(la, lbv) + _log1p_exp_neg_abs(la - lbv)
    assert HG_BLOCK == 2 * SUBLANES
    g2 = _pair_tile_cumsum(logf) * LOG2E
    z_ref[:, O_MG:O_MG + D_MODEL] = g2
    z_ref[:, O_HF:O_HF + D_MODEL] = g2 - jnp.log((1.0 - lb) * _sigmoid(-hf)) * LOG2E
    z_ref[:, O_HQ:O_HQ + D_MODEL] = _silu(z_ref[:, O_HQ:O_HQ + D_MODEL]) * (HG_DK ** -0.5)
    half = HG_BLOCK // 2
    rowi = lax.broadcasted_iota(jnp.int32, (HG_BLOCK, 1), 0)
    kcat = HG_BLOCK * HG_DK
    sel_j = (lax.broadcasted_iota(jnp.int32, (HG_BLOCK, kcat), 0)
             == lax.shift_right_logical(lax.broadcasted_iota(jnp.int32, (HG_BLOCK, kcat), 1), 7)).astype(BF16)
    pair_mask = (lax.shift_right_logical(lax.broadcasted_iota(jnp.int32, (2 * HG_BLOCK, 2 * HG_DK), 0), 4)
                 == lax.shift_right_logical(lax.broadcasted_iota(jnp.int32, (2 * HG_BLOCK, 2 * HG_DK), 1), 7))
    zero_half = jnp.zeros((half, HG_DK), F32)
    zero_rows = jnp.zeros((LANES - HG_BLOCK, HG_H * HG_BLOCK), F32)

    ub = min(HG_UNROLL, tb // HG_BLOCK)

    def intra_group(r0):
        pcats = []
        for u in range(ub):
            rows = pl.ds(r0 + u * HG_BLOCK, HG_BLOCK)
            gb = z_ref[rows, O_MG:O_MG + D_MODEL]
            hb = z_ref[rows, O_HF:O_HF + D_MODEL]
            qb = z_ref[rows, O_HQ:O_HQ + D_MODEL]
            pieces = []
            for j in range(HG_BLOCK):
                lo = 0 if j < half else half
                pj = qb[lo:] * jnp.exp2(jnp.where(rowi[lo:] >= j, gb[lo:] - hb[j:j + 1, :], -jnp.inf))
                tiles = []
                for h in range(HG_H):
                    if lo:
                        tiles.append(zero_half)
                    tiles.append(pj[:, h * HG_DK:(h + 1) * HG_DK])
                pieces.append(jnp.concatenate(tiles, axis=0))
            pcats.append(jnp.concatenate(pieces, axis=1).astype(BF16))
        atts = []
        for u in range(ub):
            att_t = lax.dot_general(sel_j, pcats[u], (((1,), (1,)), ((), ())),
                                    preferred_element_type=F32)
            atts.append(jnp.concatenate([att_t, zero_rows], axis=0).T)
        for u in range(ub):
            rows = pl.ds(r0 + u * HG_BLOCK, HG_BLOCK)
            vb = z_ref[rows, O_HI:O_HI + D_MODEL]
            for h in range(HG_H):
                att_h = atts[u][h * HG_BLOCK:(h + 1) * HG_BLOCK, 0:HG_BLOCK]
                oi_ref[rows, h * HG_DV:(h + 1) * HG_DV] = _dot(att_h, vb[:, h * HG_DV:(h + 1) * HG_DV])

    def state_group(r0):
        for u in range(ub):
            rows = pl.ds(r0 + u * HG_BLOCK, HG_BLOCK)
            gb = z_ref[rows, O_MG:O_MG + D_MODEL]
            gl = gb[HG_BLOCK - 1:HG_BLOCK, :]
            egl = jnp.exp2(gl)
            qe = z_ref[rows, O_HQ:O_HQ + D_MODEL] * jnp.exp2(gb)
            ke = jnp.exp2(gl - z_ref[rows, O_HF:O_HF + D_MODEL])
            vb = z_ref[rows, O_HI:O_HI + D_MODEL]
            for h in range(HG_H):
                ly = slice(D_MODEL + h * HG_DV, D_MODEL + (h + 1) * HG_DV)
                st_old = st_ref[h // 2, :, (h % 2) * HG_DK:(h % 2 + 1) * HG_DK]
                o = oi_ref[rows, h * HG_DV:(h + 1) * HG_DV] + _dot_nt(qe[:, h * HG_DK:(h + 1) * HG_DK], st_old)
                on = o * lax.rsqrt(jnp.mean(o * o, axis=-1, keepdims=True) + EPS) * hng_ref[...]
                lgt = slice(O_HGATE + h * HG_DV, O_HGATE + (h + 1) * HG_DV)
                y_ref[rows, ly] = on * _silu(z_ref[rows, lgt])
            for p in range(hp):
                lp = slice(2 * p * HG_DK, (2 * p + 2) * HG_DK)
                vst = jnp.concatenate([vb[:, 2 * p * HG_DV:(2 * p + 1) * HG_DV],
                                       vb[:, (2 * p + 1) * HG_DV:(2 * p + 2) * HG_DV]], axis=0)
                kep = ke[:, lp]
                rhs = jnp.where(pair_mask, jnp.concatenate([kep, kep], axis=0), 0.0)
                st_ref[p] = egl[:, lp] * st_ref[p] + _dot_tn(vst, rhs)

    grp = ub * HG_BLOCK
    ngrp = tb // grp
    intra_group(0)

    def pipe_body(gi, carry):
        r0 = pl.multiple_of(gi * grp, grp)
        state_group(r0)
        intra_group(r0 + grp)
        return carry

    lax.fori_loop(0, ngrp - 1, pipe_body, 0)
    state_group((ngrp - 1) * grp)

    y = jnp.dot(y_ref[...].astype(BF16), wout_ref[...], preferred_element_type=F32)
    xo_ref[...] = _layer_norm_rows(ALPHA * x + y, lng_ref[...], lnb_ref[...])

    @pl.when(t == nt - 1)
    def _fin():
        co_ref[...] = c_ref[...]
        no_ref[...] = n_ref[...]
        mo_ref[...] = m_ref[...]
        for h in range(HG_H):
            hso_ref[h] = st_ref[h // 2, :, (h % 2) * HG_DK:(h % 2 + 1) * HG_DK].T


def _const_spec(shape):
    nd = len(shape)
    return pl.BlockSpec(shape, lambda b, t: (0,) * nd, pipeline_mode=pl.Buffered(1))


def _batch_spec(shape):
    nd = len(shape)
    return pl.BlockSpec((None,) + tuple(shape), lambda b, t: (b,) + (0,) * nd)


def _time_block(T, tb_max):
    tb = min(T, tb_max)
    while T % tb:
        tb -= 1
    return tb


def _chunk_len(tb, max_blk):
    c = min(max_blk, tb)
    while tb % c:
        c -= 1
    return c


def _even_layer(x, params, state, tb_max):
    B, T, D = x.shape
    tb = _time_block(T, tb_max)
    chunk = _chunk_len(tb, CHUNK)
    has_state = state is not None
    (win, cwa, cba, wai, bai, lam, cwb, gdp, gng, wout, lng, lnb) = params
    consts = [win, cwa, cba, wai, bai, lam, cwb, gdp, gng, wout, lng, lnb]
    in_specs = [pl.BlockSpec((None, tb, D), lambda b, t: (b, t, 0))]
    in_specs += [_const_spec(c.shape) for c in consts]
    args = [x] + consts
    state_shapes = [(1, RG_W), (CONV_W - 1, RG_W), (GD_H, GD_DK, GD_DV), (CONV_W - 1, GD_CONV)]
    if has_state:
        h0, cva0, s0, cvb0 = state
        args += [h0.reshape(B, 1, RG_W), cva0, s0, cvb0]
        in_specs += [_batch_spec(s) for s in state_shapes]
    out_shape = [jax.ShapeDtypeStruct((B, T, D), x.dtype)]
    out_shape += [jax.ShapeDtypeStruct((B,) + s, x.dtype) for s in state_shapes]
    out_specs = [pl.BlockSpec((None, tb, D), lambda b, t: (b, t, 0))]
    out_specs += [_batch_spec(s) for s in state_shapes]
    scratch = [
        pltpu.VMEM((tb, EVEN_COLS), F32),
        pltpu.VMEM((tb, 2 * D_MODEL), F32),
        pltpu.VMEM((CONV_PAD + tb, RG_W), F32),
        pltpu.VMEM((CONV_PAD + tb, GD_CONV), F32),
        pltpu.VMEM((1, RG_W), F32),
        pltpu.VMEM((GD_H, GD_DK, GD_DV), F32),
        pltpu.VMEM((tb, LANES), F32),
        pltpu.VMEM((tb, LANES), F32),
    ]
    n_items = (tb // chunk) * (GD_H * chunk // (2 * LANES))
    scratch += [pltpu.VMEM((n_items, chunk, 2 * LANES), F32) for _ in range(3)]
    outs = pl.pallas_call(
        functools.partial(_even_kernel, tb=tb, chunk=chunk, has_state=has_state),
        grid=(B, T // tb),
        in_specs=in_specs,
        out_specs=out_specs,
        out_shape=out_shape,
        scratch_shapes=scratch,
        compiler_params=pltpu.CompilerParams(
            dimension_semantics=("arbitrary", "arbitrary"), vmem_limit_bytes=VMEM_LIMIT_BYTES),
        name="even_layer_state" if has_state else "even_layer",
    )(*args)
    xo, ho, cvao, so, cvbo = outs
    return xo, ho.reshape(B, RG_W), cvao, so, cvbo


def _odd_layer(x, params, layer, state, tb_max):
    B, T, D = x.shape
    tb = _time_block(T, tb_max)
    chunk = _chunk_len(tb, CHUNK)
    has_state = state is not None
    (win, mlp, mng, hlb, hng, wout, lng, lnb) = params
    consts = [win, mlp, mng, hlb, hng, wout, lng, lnb]
    in_specs = [pl.BlockSpec((None, tb, D), lambda b, t: (b, t, 0))]
    in_specs += [_const_spec(c.shape) for c in consts]
    args = [x] + consts
    state_shapes = [(ML_H, ML_DK, ML_DV), (SUBLANES, ML_DK), (SUBLANES, LANES), (HG_H, HG_DK, HG_DV)]
    if has_state:
        c0, n0, m0, hs0 = state
        n0p = jnp.zeros((B, SUBLANES, ML_DK), F32).at[:, :ML_H, :].set(n0)
        m0p = jnp.zeros((B, SUBLANES, LANES), F32).at[:, :ML_H, :].set(
            jnp.broadcast_to(m0[:, :, None], (B, ML_H, LANES)))
        args += [c0, n0p, m0p, hs0]
        in_specs += [_batch_spec(s) for s in state_shapes]
    out_shape = [jax.ShapeDtypeStruct((B, T, D), x.dtype)]
    out_shape += [jax.ShapeDtypeStruct((B,) + s, x.dtype) for s in state_shapes]
    out_specs = [pl.BlockSpec((None, tb, D), lambda b, t: (b, t, 0))]
    out_specs += [_batch_spec(s) for s in state_shapes]
    scratch = [
        pltpu.VMEM((tb, ODD_COLS), F32),
        pltpu.VMEM((tb, 2 * D_MODEL), F32),
        pltpu.VMEM((ML_H, ML_DK, ML_DV), F32),
        pltpu.VMEM((SUBLANES, ML_DK), F32),
        pltpu.VMEM((SUBLANES, LANES), F32),
        pltpu.VMEM((HG_H // 2, HG_DV, 2 * HG_DK), F32),
        pltpu.VMEM((tb, LANES), F32),
        pltpu.VMEM(((tb // chunk) * ML_H, chunk, ML_DV), F32),
        pltpu.VMEM(((tb // chunk) * ML_H, ML_DK, ML_DV), F32),
        pltpu.VMEM(((tb // chunk) * ML_H, chunk + SUBLANES, LANES), F32),
        pltpu.VMEM((tb, D_MODEL), F32),
    ]
    outs = pl.pallas_call(
        functools.partial(_odd_kernel, tb=tb, chunk=chunk, layer=layer, has_state=has_state),
        grid=(B, T // tb),
        in_specs=in_specs,
        out_specs=out_specs,
        out_shape=out_shape,
        scratch_shapes=scratch,
        compiler_params=pltpu.CompilerParams(
            dimension_semantics=("arbitrary", "arbitrary"), vmem_limit_bytes=VMEM_LIMIT_BYTES),
        name="odd_layer_state" if has_state else "odd_layer",
    )(*args)
    xo, co, no, mo, hso = outs
    return xo, co, no[:, :ML_H, :], mo[:, :ML_H, 0], hso


def _pad_lanes(cols):
    return jnp.pad(cols, ((0, 0), (0, LANES - cols.shape[1])))


def _even_params(j, w_in_even, rg_conv_w, rg_conv_b, rg_w_a, rg_b_a, rg_w_i, rg_b_i, rg_lambda,
                 gdn_conv_w, gdn_a_log, gdn_dt_bias, gdn_norm_g, w_out_even, ln_even_g, ln_even_b):
    w = w_in_even[j]
    xa, ga, qb, kb, vb = (w[:, i * 1024:(i + 1) * 1024] for i in range(5))
    small = w[:, 5120:5136]
    gb = w[:, 5136:6160]
    win = jnp.concatenate([xa, gb, ga, qb, kb, vb, _pad_lanes(small)], axis=1).astype(BF16)
    wai = jnp.concatenate([rg_w_a[j], rg_w_i[j]], axis=-1).astype(BF16)
    bai = jnp.stack([rg_b_a[j], rg_b_i[j]]).astype(F32)
    gdp = jnp.zeros((2, LANES), F32)
    gdp = gdp.at[0, GD_H:2 * GD_H].set(gdn_a_log[j]).at[1, GD_H:2 * GD_H].set(gdn_dt_bias[j])
    return (win, rg_conv_w[j], rg_conv_b[j][None, :], wai, bai, rg_lambda[j][None, :],
            gdn_conv_w[j], gdp, gdn_norm_g[j][None, :], w_out_even[j].astype(BF16),
            ln_even_g[j][None, :], ln_even_b[j][None, :])


def _odd_params(j, w_in_odd, ml_b_i, ml_b_f, ml_norm_g, hg_lb, hg_norm_g, w_out_odd, ln_odd_g, ln_odd_b):
    w = w_in_odd[j]
    mq, mk, mv = w[:, 0:512], w[:, 512:1024], w[:, 1024:2048]
    small = w[:, 2048:2056]
    mo, mg, hq, hf, hi, hgate = (w[:, 2056 + i * 1024:2056 + (i + 1) * 1024] for i in range(6))
    win = jnp.concatenate([mo, hgate, mg, mq, mk, mv, hq, hf, hi, _pad_lanes(small)], axis=1).astype(BF16)
    mlp = jnp.zeros((2, LANES), F32)
    mlp = mlp.at[0, 0:ML_H].set(ml_b_i[j]).at[1, ML_H:2 * ML_H].set(ml_b_f[j])
    return (win, mlp, ml_norm_g[j][None, :], hg_lb.astype(F32), hg_norm_g[j][None, :],
            w_out_odd[j].astype(BF16), ln_odd_g[j][None, :], ln_odd_b[j][None, :])


def _run(x, states, even_p, odd_p, tb_max):
    rg_h, rg_cv, gd_s, gd_cv, ml_c, ml_n, ml_m, hg_s = ([] for _ in range(8))
    for l in range(DEPTH):
        j = l // 2
        if l % 2 == 0:
            st = None if states is None else (states[0][j], states[1][j], states[2][j], states[3][j])
            x, h, cva, s, cvb = _even_layer(x, even_p[j], st, tb_max)
            rg_h.append(h); rg_cv.append(cva); gd_s.append(s); gd_cv.append(cvb)
        else:
            st = None if states is None else (states[4][j], states[5][j], states[6][j], states[7][j])
            x, c, n, m, s = _odd_layer(x, odd_p[j], j, st, tb_max)
            ml_c.append(c); ml_n.append(n); ml_m.append(m); hg_s.append(s)
    return (x, jnp.stack(rg_h), jnp.stack(rg_cv), jnp.stack(gd_s), jnp.stack(gd_cv),
            jnp.stack(ml_c), jnp.stack(ml_n), jnp.stack(ml_m), jnp.stack(hg_s))


def _forward(x_prompt, x_sample, states, even_w, odd_w, tb_max=TIME_BLOCK):
    n_even = even_w[0].shape[0]
    n_odd = odd_w[0].shape[0]
    even_p = [_even_params(j, *even_w) for j in range(n_even)]
    odd_p = [_odd_params(j, *odd_w) for j in range(n_odd)]
    p = _run(x_prompt, None, even_p, odd_p, tb_max)
    s = _run(x_sample, states, even_p, odd_p, tb_max)
    return (p[0], s[0]) + p[1:] + s[1:]


def kernel(x_prompt, x_sample, state_rglru_h, state_rglru_conv, state_gdn_S, state_gdn_conv, state_mlstm_C, state_mlstm_n, state_mlstm_m, state_hgrn_S, w_in_even, rg_conv_w, rg_conv_b, rg_w_a, rg_b_a, rg_w_i, rg_b_i, rg_lambda, gdn_conv_w, gdn_a_log, gdn_dt_bias, gdn_norm_g, w_out_even, ln_even_g, ln_even_b, w_in_odd, ml_b_i, ml_b_f, ml_norm_g, hg_lb, hg_norm_g, w_out_odd, ln_odd_g, ln_odd_b):
    states = (state_rglru_h, state_rglru_conv, state_gdn_S, state_gdn_conv,
              state_mlstm_C, state_mlstm_n, state_mlstm_m, state_hgrn_S)
    even_w = (w_in_even, rg_conv_w, rg_conv_b, rg_w_a, rg_b_a, rg_w_i, rg_b_i, rg_lambda,
              gdn_conv_w, gdn_a_log, gdn_dt_bias, gdn_norm_g, w_out_even, ln_even_g, ln_even_b)
    odd_w = (w_in_odd, ml_b_i, ml_b_f, ml_norm_g, hg_lb, hg_norm_g, w_out_odd, ln_odd_g, ln_odd_b)
    return _forward(x_prompt, x_sample, states, even_w, odd_w)
```
